```python
import jax, jax.numpy as jnp
from jax import lax
import numpy as np

D_MODEL = 2048
BATCH = 4
SEQ = 4096
DEPTH = 2

HEAD_DIM = 128
N_HEADS_DIL = D_MODEL // (2 * HEAD_DIM)
N_HEADS_SB = D_MODEL // (2 * HEAD_DIM)
D_DIL = N_HEADS_DIL * HEAD_DIM
D_SB = N_HEADS_SB * HEAD_DIM
D_MIX = D_DIL + D_SB
DILATED_BRANCHES = ((128, 1), (512, 4), (2048, 16))
Q_BLOCK = 128
N_BUCKETS = 32
MAX_DISTANCE = 2048
D_FF_DENSE = 5504
N_EXPERTS = 8
TOP_K = 2
D_FF_EXPERT = 7168
N_DENSE_LAYERS = (DEPTH + 1) // 2
N_MOE_LAYERS = DEPTH // 2
EPS = 1e-6

kernel_name = "hybrid_dilated_stickbreaking_moe_block"


def rms_norm(x, g):
    xf = x.astype(jnp.float32)
    y = xf * lax.rsqrt(jnp.mean(xf * xf, axis=-1, keepdims=True) + EPS)
    return (y * g.astype(jnp.float32)).astype(x.dtype)


def t5_bucket(dist):
    n = np.asarray(dist, dtype=np.int64)
    max_exact = N_BUCKETS // 2
    large = max_exact + (np.log(np.maximum(n, 1) / max_exact)
                         / np.log(MAX_DISTANCE / max_exact)
                         * (N_BUCKETS - max_exact)).astype(np.int64)
    large = np.minimum(large, N_BUCKETS - 1)
    return np.where(n < max_exact, n, large).astype(np.int32)


def to_blocks(t):
    b, h, s, dh = t.shape
    return t.reshape(b, h, s // Q_BLOCK, Q_BLOCK, dh).transpose(2, 0, 1, 3, 4)


def from_blocks(t):
    nb, b, h, q, dh = t.shape
    return t.transpose(1, 2, 0, 3, 4).reshape(b, h, nb * q, dh)


def dilated_attention(q, k, v, rel_bias):
    b, h, s, dh = q.shape
    nb = s // Q_BLOCK
    qb = to_blocks(q * (dh ** -0.5))
    branches = []
    for (w, d) in DILATED_BRANCHES:
        m = np.arange(w // d + 1, dtype=np.int32)
        bias = rel_bias[t5_bucket(d * m)].T.astype(jnp.float32)
        branches.append((d, m, bias))

    def block(args):
        q_blk, i = args
        t = i * Q_BLOCK + jnp.arange(Q_BLOCK, dtype=jnp.int32)
        lses, outs = [], []
        for d, m, bias in branches:
            idx = t[:, None] - d * m[None, :]
            valid = idx >= 0
            idx = jnp.maximum(idx, 0)
            k_g = k[:, :, idx]
            v_g = v[:, :, idx]
            sc = jnp.einsum('bhqd,bhqmd->bhqm', q_blk, k_g).astype(jnp.float32)
            sc = jnp.where(valid, sc + bias[None, :, None, :], -jnp.inf)
            lse = jax.nn.logsumexp(sc, axis=-1, keepdims=True)
            p = jnp.exp(sc - lse).astype(v.dtype)
            outs.append(jnp.einsum('bhqm,bhqmd->bhqd', p, v_g).astype(jnp.float32))
            lses.append(lse)
        wts = jax.nn.softmax(jnp.stack(lses, axis=0), axis=0)
        return jnp.sum(wts * jnp.stack(outs, axis=0), axis=0).astype(q.dtype)

    out = lax.map(block, (qb, jnp.arange(nb, dtype=jnp.int32)))
    return from_blocks(out)


def stick_breaking_attention(q, k, v):
    b, h, s, dh = q.shape
    nb = s // Q_BLOCK
    qb = to_blocks(q * (dh ** -0.5))
    pos = jnp.arange(s, dtype=jnp.int32)

    def block(args):
        q_blk, i = args
        t = i * Q_BLOCK + jnp.arange(Q_BLOCK, dtype=jnp.int32)
        causal = pos[None, :] < t[:, None]
        z = jnp.einsum('bhqd,bhsd->bhqs', q_blk, k).astype(jnp.float32)
        log_keep = jnp.where(causal, jax.nn.log_sigmoid(-z), 0.0)
        later = lax.cumsum(log_keep, axis=3, reverse=True) - log_keep
        a = jnp.where(causal, jnp.exp(jax.nn.log_sigmoid(z) + later), 0.0)
        return jnp.einsum('bhqs,bhsd->bhqd', a.astype(v.dtype), v)

    out = lax.map(block, (qb, jnp.arange(nb, dtype=jnp.int32)))
    return from_blocks(out)


def token_mixer(h, w_in, w_o, g_dil, g_sb, rel_bias):
    b, s, _ = h.shape
    proj = h @ w_in
    qd, kd, vd, qs, ks, vs = jnp.split(
        proj, [D_DIL, 2 * D_DIL, 3 * D_DIL, 3 * D_DIL + D_SB, 3 * D_DIL + 2 * D_SB], axis=-1)
    heads = lambda t, n: t.reshape(b, s, n, HEAD_DIM).transpose(0, 2, 1, 3)
    merge = lambda t: t.transpose(0, 2, 1, 3).reshape(b, s, -1)
    o_dil = merge(dilated_attention(heads(qd, N_HEADS_DIL), heads(kd, N_HEADS_DIL),
                                    heads(vd, N_HEADS_DIL), rel_bias))
    o_sb = merge(stick_breaking_attention(heads(qs, N_HEADS_SB), heads(ks, N_HEADS_SB),
                                          heads(vs, N_HEADS_SB)))
    o = jnp.concatenate([rms_norm(o_dil, g_dil), rms_norm(o_sb, g_sb)], axis=-1)
    return o @ w_o


def swiglu(h, w_gate, w_up, w_down):
    return (jax.nn.silu(h @ w_gate) * (h @ w_up)) @ w_down


def moe_swiglu(h, w_router, w_gate, w_up, w_down):
    logits = (h @ w_router).astype(jnp.float32)
    top_v, top_i = lax.top_k(logits, TOP_K)
    top_w = jax.nn.softmax(top_v, axis=-1)
    gates = jnp.sum(jax.nn.one_hot(top_i, N_EXPERTS, dtype=jnp.float32) * top_w[..., None], axis=-2)
    y = jnp.zeros_like(h)
    for e in range(N_EXPERTS):
        y = y + gates[..., e:e + 1].astype(h.dtype) * swiglu(h, w_gate[e], w_up[e], w_down[e])
    return y


def setup_inputs(seed: int = 0) -> dict:
    key = jax.random.key(seed)
    ks = jax.random.split(key, 18)
    nrm = lambda k, shape, scale: jax.random.normal(k, shape, jnp.float32) * scale
    gain = lambda k, shape: 1.0 + 0.02 * jax.random.normal(k, shape, jnp.float32)
    return {
        "x": nrm(ks[0], (BATCH, SEQ, D_MODEL), 1.0),
        "rel_bias": nrm(ks[1], (N_BUCKETS, N_HEADS_DIL), 0.5),
        "attn_norm_g": gain(ks[2], (DEPTH, D_MODEL)),
        "w_in": nrm(ks[3], (DEPTH, D_MODEL, 3 * D_MIX), D_MODEL ** -0.5),
        "mix_norm_dil_g": gain(ks[4], (DEPTH, D_DIL)),
        "mix_norm_sb_g": gain(ks[5], (DEPTH, D_SB)),
        "w_o": nrm(ks[6], (DEPTH, D_MIX, D_MODEL), D_MIX ** -0.5),
        "ffn_norm_g": gain(ks[7], (DEPTH, D_MODEL)),
        "dense_w_gate": nrm(ks[8], (N_DENSE_LAYERS, D_MODEL, D_FF_DENSE), D_MODEL ** -0.5),
        "dense_w_up": nrm(ks[9], (N_DENSE_LAYERS, D_MODEL, D_FF_DENSE), D_MODEL ** -0.5),
        "dense_w_down": nrm(ks[10], (N_DENSE_LAYERS, D_FF_DENSE, D_MODEL), D_FF_DENSE ** -0.5),
        "router_w": nrm(ks[11], (N_MOE_LAYERS, D_MODEL, N_EXPERTS), D_MODEL ** -0.5),
        "moe_w_gate": nrm(ks[12], (N_MOE_LAYERS, N_EXPERTS, D_MODEL, D_FF_EXPERT), D_MODEL ** -0.5),
        "moe_w_up": nrm(ks[13], (N_MOE_LAYERS, N_EXPERTS, D_MODEL, D_FF_EXPERT), D_MODEL ** -0.5),
        "moe_w_down": nrm(ks[14], (N_MOE_LAYERS, N_EXPERTS, D_FF_EXPERT, D_MODEL), D_FF_EXPERT ** -0.5),
        "final_norm_g": gain(ks[15], (D_MODEL,)),
    }


def reference(x, rel_bias, attn_norm_g, w_in, mix_norm_dil_g, mix_norm_sb_g, w_o, ffn_norm_g,
              dense_w_gate, dense_w_up, dense_w_down, router_w, moe_w_gate, moe_w_up,
              moe_w_down, final_norm_g):
    for layer in range(DEPTH):
        h = rms_norm(x, attn_norm_g[layer])
        x = x + token_mixer(h, w_in[layer], w_o[layer], mix_norm_dil_g[layer],
                            mix_norm_sb_g[layer], rel_bias)
        h = rms_norm(x, ffn_norm_g[layer])
        j = layer // 2
        if layer % 2 == 0:
            x = x + swiglu(h, dense_w_gate[j], dense_w_up[j], dense_w_down[j])
        else:
            x = x + moe_swiglu(h, router_w[j], moe_w_gate[j], moe_w_up[j], moe_w_down[j])
    return rms_norm(x, final_norm_g)
```

```python
import functools

import numpy as np
import jax
import jax.numpy as jnp
from jax import lax
from jax.experimental import pallas as pl
from jax.experimental.pallas import tpu as pltpu

F32 = jnp.float32
BF16 = jnp.bfloat16
I32 = jnp.int32

EPS = 1e-6
HEAD_DIM = 128
DILATED_BRANCHES = ((128, 1), (512, 4), (2048, 16))
N_BUCKETS = 32
MAX_DISTANCE = 2048
TOP_K = 2
LANES = 128
WIN = 128
NEG_INF = float("-inf")
MIB = 1024 * 1024


def _params(semantics, vmem_mib):
    return pltpu.CompilerParams(dimension_semantics=semantics,
                                vmem_limit_bytes=vmem_mib * MIB)


def _rms(x, g):
    return x * lax.rsqrt(jnp.mean(x * x, axis=-1, keepdims=True) + EPS) * g


def _norm_matmul_body(x_ref, g_ref, w_ref, o_ref, h_ref):
    @pl.when(pl.program_id(1) == 0)
    def _():
        h_ref[...] = _rms(x_ref[...], g_ref[...]).astype(BF16)

    o_ref[...] = jnp.dot(h_ref[...], w_ref[...],
                         preferred_element_type=F32).astype(o_ref.dtype)


def norm_matmul(x, g, w, out_dtype, tm=1024, tn=1024):
    t, d = x.shape
    n = w.shape[1]
    tm, tn = min(tm, t), min(tn, n)
    return pl.pallas_call(
        _norm_matmul_body,
        grid=(t // tm, n // tn),
        in_specs=[pl.BlockSpec((tm, d), lambda i, j: (i, 0)),
                  pl.BlockSpec((1, d), lambda i, j: (0, 0)),
                  pl.BlockSpec((d, tn), lambda i, j: (0, j))],
        out_specs=pl.BlockSpec((tm, tn), lambda i, j: (i, j)),
        out_shape=jax.ShapeDtypeStruct((t, n), out_dtype),
        scratch_shapes=[pltpu.VMEM((tm, d), BF16)],
        compiler_params=_params(("parallel", "arbitrary"), 48),
        name="norm_matmul",
    )(x, g, w)


def _silu_mul(a, b):
    return a * (1.0 / (1.0 + jnp.exp(-a))) * b


def _ffn_up_body(x_ref, g_ref, wg_ref, wu_ref, o_ref, h_ref):
    @pl.when(pl.program_id(1) == 0)
    def _():
        h_ref[...] = _rms(x_ref[...], g_ref[...]).astype(BF16)

    h = h_ref[...]
    a = jnp.dot(h, wg_ref[...], preferred_element_type=F32)
    b = jnp.dot(h, wu_ref[...], preferred_element_type=F32)
    o_ref[...] = _silu_mul(a, b).astype(o_ref.dtype)


def ffn_up(x, g, wg, wu, tm=1024, tn=512):
    t, d = x.shape
    f = wg.shape[1]
    tm, tn = min(tm, t), min(tn, f)
    return pl.pallas_call(
        _ffn_up_body,
        grid=(t // tm, pl.cdiv(f, tn)),
        in_specs=[pl.BlockSpec((tm, d), lambda i, j: (i, 0)),
                  pl.BlockSpec((1, d), lambda i, j: (0, 0)),
                  pl.BlockSpec((d, tn), lambda i, j: (0, j)),
                  pl.BlockSpec((d, tn), lambda i, j: (0, j))],
        out_specs=pl.BlockSpec((tm, tn), lambda i, j: (i, j)),
        out_shape=jax.ShapeDtypeStruct((t, f), BF16),
        scratch_shapes=[pltpu.VMEM((tm, d), BF16)],
        compiler_params=_params(("parallel", "arbitrary"), 48),
        name="ffn_up",
    )(x, g, wg, wu)


def _matmul_res_body(a_ref, w_ref, r_ref, o_ref):
    o_ref[...] = r_ref[...] + jnp.dot(a_ref[...], w_ref[...], preferred_element_type=F32)


def matmul_residual(a, w, res, tm=512, tn=1024):
    t, f = a.shape
    n = w.shape[1]
    tm, tn = min(tm, t), min(tn, n)
    return pl.pallas_call(
        _matmul_res_body,
        grid=(n // tn, t // tm),
        in_specs=[pl.BlockSpec((tm, f), lambda j, i: (i, 0)),
                  pl.BlockSpec((f, tn), lambda j, i: (0, j)),
                  pl.BlockSpec((tm, tn), lambda j, i: (i, j))],
        out_specs=pl.BlockSpec((tm, tn), lambda j, i: (i, j)),
        out_shape=jax.ShapeDtypeStruct((t, n), F32),
        compiler_params=_params(("parallel", "parallel"), 52),
        name="matmul_residual",
    )(a, w, res)


def _mix_out_body(od_ref, os_ref, gd_ref, gs_ref, w_ref, x_ref, o_ref, h_ref):
    dd = od_ref.shape[1]

    @pl.when(pl.program_id(1) == 0)
    def _():
        h_ref[:, :dd] = _rms(od_ref[...], gd_ref[...]).astype(BF16)
        h_ref[:, dd:] = _rms(os_ref[...], gs_ref[...]).astype(BF16)

    o_ref[...] = x_ref[...] + jnp.dot(h_ref[...], w_ref[...], preferred_element_type=F32)


def mix_out(o_dil, o_sb, g_dil, g_sb, w_o, x, tm=512, tn=1024):
    t, dd = o_dil.shape
    ds = o_sb.shape[1]
    n = w_o.shape[1]
    tm, tn = min(tm, t), min(tn, n)
    return pl.pallas_call(
        _mix_out_body,
        grid=(t // tm, n // tn),
        in_specs=[pl.BlockSpec((tm, dd), lambda i, j: (i, 0)),
                  pl.BlockSpec((tm, ds), lambda i, j: (i, 0)),
                  pl.BlockSpec((1, dd), lambda i, j: (0, 0)),
                  pl.BlockSpec((1, ds), lambda i, j: (0, 0)),
                  pl.BlockSpec((dd + ds, tn), lambda i, j: (0, j)),
                  pl.BlockSpec((tm, tn), lambda i, j: (i, j))],
        out_specs=pl.BlockSpec((tm, tn), lambda i, j: (i, j)),
        out_shape=jax.ShapeDtypeStruct((t, n), F32),
        scratch_shapes=[pltpu.VMEM((tm, dd + ds), BF16)],
        compiler_params=_params(("parallel", "arbitrary"), 48),
        name="mix_out",
    )(o_dil, o_sb, g_dil, g_sb, w_o, x)


def _t5_bucket(dist):
    n = np.asarray(dist, dtype=np.int64)
    max_exact = N_BUCKETS // 2
    large = max_exact + (np.log(np.maximum(n, 1) / max_exact)
                         / np.log(MAX_DISTANCE / max_exact)
                         * (N_BUCKETS - max_exact)).astype(np.int64)
    large = np.minimum(large, N_BUCKETS - 1)
    return np.where(n < max_exact, n, large).astype(np.int32)


def _dilated_bias_tables(rel_bias):
    i = np.arange(WIN)[:, None]
    j = np.arange(2 * WIN)[None, :]
    m = i + WIN - j
    valid = (m >= 0) & (m <= WIN)
    mc = np.clip(m, 0, WIN)
    tables = []
    for (w, d) in DILATED_BRANCHES:
        assert w // d == WIN
        bias_m = rel_bias[_t5_bucket(d * np.arange(WIN + 1))].astype(F32)
        tb = jnp.where(valid[:, :, None], bias_m[mc], NEG_INF)
        tables.append(jnp.transpose(tb, (2, 0, 1)))
    return jnp.stack(tables, axis=1)


def _dilated_body(q_ref, k_ref, v_ref, tb_ref, o_ref, ob_ref, lb_ref):
    s = q_ref.shape[0]
    scale = HEAD_DIM ** -0.5
    col = lax.broadcasted_iota(I32, (WIN, 2 * WIN), 1)

    for bi, (_, d) in enumerate(DILATED_BRANCHES):
        nblk = s // (d * WIN)

        def block(it, carry, bi=bi, d=d, nblk=nblk):
            r = it // nblk
            blk = it % nblk
            base = r + d * WIN * blk
            prev = jnp.maximum(base - d * WIN, r)
            q = (q_ref[pl.ds(base, WIN, stride=d), :] * scale).astype(BF16)
            kk = jnp.concatenate([k_ref[pl.ds(prev, WIN, stride=d), :],
                                  k_ref[pl.ds(base, WIN, stride=d), :]], axis=0).astype(BF16)
            vv = jnp.concatenate([v_ref[pl.ds(prev, WIN, stride=d), :],
                                  v_ref[pl.ds(base, WIN, stride=d), :]], axis=0).astype(BF16)
            sc = lax.dot_general(q, kk, (((1,), (1,)), ((), ())),
                                 preferred_element_type=F32) + tb_ref[bi]
            sc = jnp.where((col < WIN) & (blk == 0), NEG_INF, sc)
            m = jnp.max(sc, axis=-1, keepdims=True)
            p = jnp.exp(sc - m)
            l = jnp.sum(p, axis=-1, keepdims=True)
            out = jnp.dot(p.astype(BF16), vv, preferred_element_type=F32) * (1.0 / l)
            lse = m + jnp.log(l)
            ob_ref[bi, pl.ds(base, WIN, stride=d), :] = out
            lb_ref[bi, pl.ds(base, WIN, stride=d), :] = jnp.broadcast_to(lse, (WIN, LANES))
            return carry

        lax.fori_loop(0, d * nblk, block, 0)

    ch = 256
    def merge(c, carry):
        rows = pl.ds(pl.multiple_of(c * ch, ch), ch)
        ls = [lb_ref[b, rows, :] for b in range(len(DILATED_BRANCHES))]
        mx = functools.reduce(jnp.maximum, ls)
        es = [jnp.exp(x - mx) for x in ls]
        inv = 1.0 / functools.reduce(jnp.add, es)
        acc = es[0] * inv * ob_ref[0, rows, :]
        for b in range(1, len(DILATED_BRANCHES)):
            acc = acc + es[b] * inv * ob_ref[b, rows, :]
        o_ref[rows, :] = acc
        return carry

    lax.fori_loop(0, s // ch, merge, 0)


def dilated_attention(proj, tables, n_heads):
    b, s, _ = proj.shape
    h = n_heads
    nbr = len(DILATED_BRANCHES)
    return pl.pallas_call(
        _dilated_body,
        grid=(b, h),
        in_specs=[pl.BlockSpec((None, s, HEAD_DIM), lambda bi, hi: (bi, 0, hi)),
                  pl.BlockSpec((None, s, HEAD_DIM), lambda bi, hi: (bi, 0, h + hi)),
                  pl.BlockSpec((None, s, HEAD_DIM), lambda bi, hi: (bi, 0, 2 * h + hi)),
                  pl.BlockSpec((None, nbr, WIN, 2 * WIN), lambda bi, hi: (hi, 0, 0, 0))],
        out_specs=pl.BlockSpec((None, s, HEAD_DIM), lambda bi, hi: (bi, 0, hi)),
        out_shape=jax.ShapeDtypeStruct((b, s, h * HEAD_DIM), F32),
        scratch_shapes=[pltpu.VMEM((nbr, s, HEAD_DIM), F32),
                        pltpu.VMEM((nbr, s, LANES), F32)],
        compiler_params=_params(("parallel", "parallel"), 48),
        name="dilated_attention",
    )(proj, proj, proj, tables)


SB_BLOCK = 256


def _softplus(z):
    return jnp.maximum(z, 0.0) + jnp.log(1.0 + jnp.exp(-jnp.abs(z)))


def _suffix_sum(sp, tri):
    hi = sp.astype(BF16)
    lo = (sp - hi.astype(F32)).astype(BF16)
    return (jnp.dot(hi, tri, preferred_element_type=F32)
            + jnp.dot(lo, tri, preferred_element_type=F32))


def _sb_body(q_ref, k_ref, v_ref, o_ref, acc_ref, c_ref):
    blk = SB_BLOCK
    qi = pl.program_id(2)
    scale = HEAD_DIM ** -0.5
    q = (q_ref[...].astype(F32) * scale).astype(BF16)
    row = lax.broadcasted_iota(I32, (blk, blk), 0)
    col = lax.broadcasted_iota(I32, (blk, blk), 1)
    tri = (row >= col).astype(BF16)
    causal = col < row

    def scores(kb):
        rows = pl.ds(pl.multiple_of(kb * blk, blk), blk)
        z = lax.dot_general(q, k_ref[rows, :], (((1,), (1,)), ((), ())),
                            preferred_element_type=F32)
        return z, v_ref[rows, :]

    z, v = scores(qi)
    sp = jnp.where(causal, _softplus(z), 0.0)
    ssum = _suffix_sum(sp, tri)
    a = jnp.where(causal, jnp.exp(z - ssum), 0.0)
    acc_ref[...] = jnp.dot(a.astype(BF16), v, preferred_element_type=F32)
    c_ref[...] = jnp.broadcast_to(jnp.sum(sp, axis=-1, keepdims=True), (blk, LANES))

    def step(it, carry):
        z, v = scores(qi - 1 - it)
        sp = _softplus(z)
        ssum = _suffix_sum(sp, tri)
        c = c_ref[...]
        a = jnp.exp(z - ssum - jnp.concatenate([c, c], axis=1))
        acc_ref[...] += jnp.dot(a.astype(BF16), v, preferred_element_type=F32)
        c_ref[...] = c + jnp.broadcast_to(jnp.sum(sp, axis=-1, keepdims=True), (blk, LANES))
        return carry

    lax.fori_loop(0, qi, step, 0)
    o_ref[...] = acc_ref[...]


def stick_breaking_attention(proj, n_heads):
    b, s, _ = proj.shape
    h = n_heads
    blk = SB_BLOCK
    assert LANES * 2 == blk
    return pl.pallas_call(
        _sb_body,
        grid=(b, h, s // blk),
        in_specs=[pl.BlockSpec((None, blk, HEAD_DIM), lambda bi, hi, qi: (bi, qi, hi)),
                  pl.BlockSpec((None, s, HEAD_DIM), lambda bi, hi, qi: (bi, 0, h + hi)),
                  pl.BlockSpec((None, s, HEAD_DIM), lambda bi, hi, qi: (bi, 0, 2 * h + hi))],
        out_specs=pl.BlockSpec((None, blk, HEAD_DIM), lambda bi, hi, qi: (bi, qi, hi)),
        out_shape=jax.ShapeDtypeStruct((b, s, h * HEAD_DIM), F32),
        scratch_shapes=[pltpu.VMEM((blk, HEAD_DIM), F32),
                        pltpu.VMEM((blk, LANES), F32)],
        compiler_params=_params(("parallel", "parallel", "parallel"), 32),
        name="stick_breaking_attention",
    )(proj, proj, proj)


MOE_TILE = 512
META_I1, META_I2, META_W1, META_W2, META_P1, META_P2 = range(6)


def _router_body(n_experts, x_ref, g_ref, rw_ref, h_ref, meta_ref, cnt_ref, carry_ref):
    tm = x_ref.shape[0]

    @pl.when(pl.program_id(0) == 0)
    def _():
        carry_ref[...] = jnp.zeros_like(carry_ref)

    h = _rms(x_ref[...], g_ref[...])
    h_ref[...] = h
    logits = jnp.dot(h, rw_ref[...], precision=lax.Precision.HIGHEST,
                     preferred_element_type=F32)
    lane = lax.broadcasted_iota(I32, logits.shape, 1)
    l1 = jnp.where(lane < n_experts, logits, NEG_INF)
    m1 = jnp.max(l1, axis=-1, keepdims=True)
    i1 = jnp.min(jnp.where(l1 == m1, lane, LANES), axis=-1, keepdims=True)
    l2 = jnp.where(lane == i1, NEG_INF, l1)
    m2 = jnp.max(l2, axis=-1, keepdims=True)
    i2 = jnp.min(jnp.where(l2 == m2, lane, LANES), axis=-1, keepdims=True)
    e2 = jnp.exp(m2 - m1)
    inv = 1.0 / (1.0 + e2)
    w1, w2 = inv, e2 * inv
    sel1, sel2 = lane == i1, lane == i2
    onehot = jnp.where(sel1 | sel2, 1.0, 0.0)
    r = lax.broadcasted_iota(I32, (tm, tm), 0)
    c = lax.broadcasted_iota(I32, (tm, tm), 1)
    before = (c < r).astype(BF16)
    pos = jnp.dot(before, onehot.astype(BF16), preferred_element_type=F32) + carry_ref[...]
    p1 = jnp.sum(jnp.where(sel1, pos, 0.0), axis=-1, keepdims=True)
    p2 = jnp.sum(jnp.where(sel2, pos, 0.0), axis=-1, keepdims=True)
    meta = jnp.zeros_like(logits)
    for idx, val in ((META_I1, i1.astype(F32)), (META_I2, i2.astype(F32)), (META_W1, w1),
                     (META_W2, w2), (META_P1, p1), (META_P2, p2)):
        meta = jnp.where(lane == idx, val, meta)
    meta_ref[...] = meta
    carry_ref[...] += jnp.sum(onehot, axis=0, keepdims=True)
    cnt_ref[...] = carry_ref[...]


def moe_router(x, g, rw, n_experts, tm=512):
    t, d = x.shape
    tm = min(tm, t)
    return pl.pallas_call(
        functools.partial(_router_body, n_experts),
        grid=(t // tm,),
        in_specs=[pl.BlockSpec((tm, d), lambda i: (i, 0)),
                  pl.BlockSpec((1, d), lambda i: (0, 0)),
                  pl.BlockSpec((d, LANES), lambda i: (0, 0))],
        out_specs=[pl.BlockSpec((tm, d), lambda i: (i, 0)),
                   pl.BlockSpec((tm, LANES), lambda i: (i, 0)),
                   pl.BlockSpec((1, LANES), lambda i: (0, 0))],
        out_shape=[jax.ShapeDtypeStruct((t, d), F32),
                   jax.ShapeDtypeStruct((t, LANES), F32),
                   jax.ShapeDtypeStruct((1, LANES), F32)],
        scratch_shapes=[pltpu.VMEM((1, LANES), F32)],
        compiler_params=_params(("arbitrary",), 32),
        name="moe_router",
    )(x, g, rw)


def _row_copy(src_hbm, src_row, dst_ref, dst_row, sem):
    return pltpu.make_async_copy(src_hbm.at[pl.ds(src_row, 1), :],
                                 dst_ref.at[pl.ds(dst_row, 1), :], sem)


def _gather_body(tok_ref, h_hbm, o_ref, sem):
    tg = o_ref.shape[0]
    base = pl.program_id(0) * tg

    def issue(r, carry):
        _row_copy(h_hbm, tok_ref[base + r], o_ref, r, sem).start()
        return carry

    def wait(r, carry):
        _row_copy(h_hbm, 0, o_ref, r, sem).wait()
        return carry

    lax.fori_loop(0, tg, issue, 0)
    lax.fori_loop(0, tg, wait, 0)


def gather_rows(h, tok, tg=256):
    t, d = h.shape
    rows = tok.shape[0]
    return pl.pallas_call(
        _gather_body,
        grid_spec=pltpu.PrefetchScalarGridSpec(
            num_scalar_prefetch=1,
            grid=(rows // tg,),
            in_specs=[pl.BlockSpec(memory_space=pl.ANY)],
            out_specs=pl.BlockSpec((tg, d), lambda i, tok: (i, 0)),
            scratch_shapes=[pltpu.SemaphoreType.DMA(())]),
        out_shape=jax.ShapeDtypeStruct((rows, d), h.dtype),
        compiler_params=_params(("arbitrary",), 32),
        name="moe_gather",
    )(tok, h)


def _moe_up_body(te_ref, nv_ref, x_ref, wg_ref, wu_ref, o_ref, h_ref):
    i, j = pl.program_id(0), pl.program_id(1)

    @pl.when(i < nv_ref[0])
    def _():
        @pl.when(j == 0)
        def _():
            h_ref[...] = x_ref[...].astype(BF16)

        h = h_ref[...]
        a = jnp.dot(h, wg_ref[...], preferred_element_type=F32)
        b = jnp.dot(h, wu_ref[...], preferred_element_type=F32)
        o_ref[...] = _silu_mul(a, b).astype(o_ref.dtype)

    @pl.when(i >= nv_ref[0])
    def _():
        o_ref[...] = jnp.zeros_like(o_ref)


def moe_up(xs, tile_expert, n_valid, wg, wu, tn=512):
    rows, d = xs.shape
    f = wg.shape[2]
    tm = MOE_TILE
    return pl.pallas_call(
        _moe_up_body,
        grid_spec=pltpu.PrefetchScalarGridSpec(
            num_scalar_prefetch=2,
            grid=(rows // tm, f // tn),
            in_specs=[pl.BlockSpec((tm, d), lambda i, j, te, nv: (i, 0)),
                      pl.BlockSpec((None, d, tn), lambda i, j, te, nv: (te[i], 0, j)),
                      pl.BlockSpec((None, d, tn), lambda i, j, te, nv: (te[i], 0, j))],
            out_specs=pl.BlockSpec((tm, tn), lambda i, j, te, nv: (i, j)),
            scratch_shapes=[pltpu.VMEM((tm, d), BF16)]),
        out_shape=jax.ShapeDtypeStruct((rows, f), BF16),
        compiler_params=_params(("parallel", "arbitrary"), 48),
        name="moe_up",
    )(tile_expert, n_valid, xs, wg, wu)


def _moe_down_body(te_ref, nv_ref, a_ref, w_ref, o_ref):
    i = pl.program_id(0)

    @pl.when(i < nv_ref[0])
    def _():
        o_ref[...] = jnp.dot(a_ref[...], w_ref[...], preferred_element_type=F32)

    @pl.when(i >= nv_ref[0])
    def _():
        o_ref[...] = jnp.zeros_like(o_ref)


def moe_down(act, tile_expert, n_valid, wd, tn=512):
    rows, f = act.shape
    d = wd.shape[2]
    tm = MOE_TILE
    return pl.pallas_call(
        _moe_down_body,
        grid_spec=pltpu.PrefetchScalarGridSpec(
            num_scalar_prefetch=2,
            grid=(rows // tm, d // tn),
            in_specs=[pl.BlockSpec((tm, f), lambda i, j, te, nv: (i, 0)),
                      pl.BlockSpec((None, f, tn), lambda i, j, te, nv: (te[i], 0, j))],
            out_specs=pl.BlockSpec((tm, tn), lambda i, j, te, nv: (i, j))),
        out_shape=jax.ShapeDtypeStruct((rows, d), F32),
        compiler_params=_params(("parallel", "arbitrary"), 48),
        name="moe_down",
    )(tile_expert, n_valid, act, wd)


def _combine_body(d1_ref, d2_ref, x_ref, meta_ref, g_ref, y_hbm, o_ref, buf_ref, sem):
    tc = x_ref.shape[0]
    base = pl.program_id(0) * tc

    def issue(r, carry):
        _row_copy(y_hbm, d1_ref[base + r], buf_ref.at[0], r, sem).start()
        _row_copy(y_hbm, d2_ref[base + r], buf_ref.at[1], r, sem).start()
        return carry

    def wait(r, carry):
        _row_copy(y_hbm, 0, buf_ref.at[0], r, sem).wait()
        _row_copy(y_hbm, 0, buf_ref.at[1], r, sem).wait()
        return carry

    lax.fori_loop(0, tc, issue, 0)
    lax.fori_loop(0, tc, wait, 0)
    meta = meta_ref[...]
    w1 = meta[:, META_W1:META_W1 + 1]
    w2 = meta[:, META_W2:META_W2 + 1]
    y = x_ref[...] + (w1 * buf_ref[0] + w2 * buf_ref[1])
    o_ref[...] = _rms(y, g_ref[...])


def moe_combine_norm(x, meta, d1, d2, y, g, tc=256):
    t, d = x.shape
    tc = min(tc, t)
    return pl.pallas_call(
        _combine_body,
        grid_spec=pltpu.PrefetchScalarGridSpec(
            num_scalar_prefetch=2,
            grid=(t // tc,),
            in_specs=[pl.BlockSpec((tc, d), lambda i, a, b: (i, 0)),
                      pl.BlockSpec((tc, LANES), lambda i, a, b: (i, 0)),
                      pl.BlockSpec((1, d), lambda i, a, b: (0, 0)),
                      pl.BlockSpec(memory_space=pl.ANY)],
            out_specs=pl.BlockSpec((tc, d), lambda i, a, b: (i, 0)),
            scratch_shapes=[pltpu.VMEM((2, tc, d), F32),
                            pltpu.SemaphoreType.DMA(())]),
        out_shape=jax.ShapeDtypeStruct((t, d), F32),
        compiler_params=_params(("arbitrary",), 32),
        name="moe_combine_norm",
    )(d1, d2, x, meta, g, y)


def moe_layer_and_final_norm(x, g, router_w, w_gate, w_up, w_down, final_g):
    t, d = x.shape
    n_experts = router_w.shape[1]
    tm = MOE_TILE
    rw = jnp.zeros((d, LANES), F32).at[:, :n_experts].set(router_w)
    h, meta, cnt = moe_router(x, g, rw, n_experts)

    rows = t * TOP_K + n_experts * tm
    n_tiles = rows // tm
    counts = cnt[0, :n_experts].astype(I32)
    tiles_per = (counts + tm - 1) // tm
    tile_end = jnp.cumsum(tiles_per)
    offs = (tile_end - tiles_per) * tm
    n_valid = tile_end[-1:]
    tile_expert = jnp.minimum(
        jnp.searchsorted(tile_end, jnp.arange(n_tiles, dtype=I32), side="right"),
        n_experts - 1).astype(I32)
    i1 = meta[:, META_I1].astype(I32)
    i2 = meta[:, META_I2].astype(I32)
    d1 = offs[i1] + meta[:, META_P1].astype(I32)
    d2 = offs[i2] + meta[:, META_P2].astype(I32)
    tid = jnp.arange(t, dtype=I32)
    tok = jnp.zeros((rows,), I32).at[jnp.concatenate([d1, d2])].set(jnp.concatenate([tid, tid]))

    xs = gather_rows(h, tok)
    act = moe_up(xs, tile_expert, n_valid, w_gate, w_up)
    y = moe_down(act, tile_expert, n_valid, w_down)
    return moe_combine_norm(x, meta, d1, d2, y, final_g)


def kernel(x, rel_bias, attn_norm_g, w_in, mix_norm_dil_g, mix_norm_sb_g, w_o, ffn_norm_g,
           dense_w_gate, dense_w_up, dense_w_down, router_w, moe_w_gate, moe_w_up,
           moe_w_down, final_norm_g):
    b, s, d = x.shape
    depth = w_in.shape[0]
    assert depth == 2, "layer 0 dense SwiGLU, layer 1 routed experts followed by the final norm"
    d_dil = mix_norm_dil_g.shape[1]
    d_sb = mix_norm_sb_g.shape[1]
    h_dil, h_sb = d_dil // HEAD_DIM, d_sb // HEAD_DIM
    tables = _dilated_bias_tables(rel_bias)
    row = lambda v: v.reshape(1, -1)

    xt = x.reshape(b * s, d)
    for layer in range(depth):
        w_in_l = w_in[layer].astype(BF16)
        g_attn = row(attn_norm_g[layer])
        p_dil = norm_matmul(xt, g_attn, w_in_l[:, :3 * d_dil], F32)
        p_sb = norm_matmul(xt, g_attn, w_in_l[:, 3 * d_dil:], BF16)
        o_dil = dilated_attention(p_dil.reshape(b, s, 3 * d_dil), tables, h_dil)
        o_sb = stick_breaking_attention(p_sb.reshape(b, s, 3 * d_sb), h_sb)
        xt = mix_out(o_dil.reshape(b * s, d_dil), o_sb.reshape(b * s, d_sb),
                     row(mix_norm_dil_g[layer]), row(mix_norm_sb_g[layer]),
                     w_o[layer].astype(BF16), xt)
        g_ffn = row(ffn_norm_g[layer])
        j = layer // 2
        if layer % 2 == 0:
            act = ffn_up(xt, g_ffn, dense_w_gate[j].astype(BF16), dense_w_up[j].astype(BF16))
            xt = matmul_residual(act, dense_w_down[j].astype(BF16), xt)
        else:
            xt = moe_layer_and_final_norm(
                xt, g_ffn, router_w[j], moe_w_gate[j].astype(BF16), moe_w_up[j].astype(BF16),
                moe_w_down[j].astype(BF16), row(final_norm_g))
    return xt.reshape(b, s, d)
```

```python
import functools

import numpy as np
import jax
import jax.numpy as jnp
from jax import lax
from jax.experimental import pallas as pl
from jax.experimental.pallas import tpu as pltpu

F32 = jnp.float32
BF16 = jnp.bfloat16
I32 = jnp.int32

EPS = 1e-6
HEAD_DIM = 128
DILATED_BRANCHES = ((128, 1), (512, 4), (2048, 16))
N_BUCKETS = 32
MAX_DISTANCE = 2048
TOP_K = 2
LANES = 128
WIN = 128
NEG_INF = float("-inf")
DIL_UNROLL = 8
MIB = 1024 * 1024


def _params(semantics, vmem_mib):
    return pltpu.CompilerParams(dimension_semantics=semantics,
                                vmem_limit_bytes=vmem_mib * MIB)


def _rms(x, g):
    return x * lax.rsqrt(jnp.mean(x * x, axis=-1, keepdims=True) + EPS) * g


def _norm_matmul_body(n_first, x_ref, g_ref, w_ref, oa_ref, ob_ref, h_ref):
    j = pl.program_id(1)

    @pl.when(j == 0)
    def _():
        h_ref[...] = _rms(x_ref[...], g_ref[...]).astype(BF16)

    y = jnp.dot(h_ref[...], w_ref[...], preferred_element_type=F32)

    @pl.when(j < n_first)
    def _():
        oa_ref[...] = y.astype(oa_ref.dtype)

    @pl.when(j >= n_first)
    def _():
        ob_ref[...] = y.astype(ob_ref.dtype)


def norm_matmul_split(x, g, w, n_a, dtype_a, dtype_b, tm=1024, tn=1024):
    t, d = x.shape
    n = w.shape[1]
    tm, tn = min(tm, t), min(tn, n_a, n - n_a)
    assert n_a % tn == 0 and (n - n_a) % tn == 0
    ja = n_a // tn
    return pl.pallas_call(
        functools.partial(_norm_matmul_body, ja),
        grid=(t // tm, n // tn),
        in_specs=[pl.BlockSpec((tm, d), lambda i, j: (i, 0)),
                  pl.BlockSpec((1, d), lambda i, j: (0, 0)),
                  pl.BlockSpec((d, tn), lambda i, j: (0, j))],
        out_specs=[pl.BlockSpec((tm, tn), lambda i, j: (i, jnp.minimum(j, ja - 1))),
                   pl.BlockSpec((tm, tn), lambda i, j: (i, jnp.maximum(j - ja, 0)))],
        out_shape=[jax.ShapeDtypeStruct((t, n_a), dtype_a),
                   jax.ShapeDtypeStruct((t, n - n_a), dtype_b)],
        scratch_shapes=[pltpu.VMEM((tm, d), BF16)],
        compiler_params=_params(("parallel", "arbitrary"), 56),
        name="norm_matmul",
    )(x, g, w)


def _silu_mul(a, b):
    return a * (1.0 / (1.0 + jnp.exp(-a))) * b


def _ffn_up_body(x_ref, g_ref, wg_ref, wu_ref, o_ref, h_ref):
    @pl.when(pl.program_id(1) == 0)
    def _():
        h_ref[...] = _rms(x_ref[...], g_ref[...]).astype(BF16)

    h = h_ref[...]
    a = jnp.dot(h, wg_ref[...], preferred_element_type=F32)
    b = jnp.dot(h, wu_ref[...], preferred_element_type=F32)
    o_ref[...] = _silu_mul(a, b).astype(o_ref.dtype)


def ffn_up(x, g, wg, wu, tm=1024, tn=512):
    t, d = x.shape
    f = wg.shape[1]
    tm, tn = min(tm, t), min(tn, f)
    return pl.pallas_call(
        _ffn_up_body,
        grid=(t // tm, pl.cdiv(f, tn)),
        in_specs=[pl.BlockSpec((tm, d), lambda i, j: (i, 0)),
                  pl.BlockSpec((1, d), lambda i, j: (0, 0)),
                  pl.BlockSpec((d, tn), lambda i, j: (0, j)),
                  pl.BlockSpec((d, tn), lambda i, j: (0, j))],
        out_specs=pl.BlockSpec((tm, tn), lambda i, j: (i, j)),
        out_shape=jax.ShapeDtypeStruct((t, f), BF16),
        scratch_shapes=[pltpu.VMEM((tm, d), BF16)],
        compiler_params=_params(("parallel", "arbitrary"), 48),
        name="ffn_up",
    )(x, g, wg, wu)


def _matmul_res_body(a_ref, w_ref, r_ref, o_ref):
    o_ref[...] = r_ref[...] + jnp.dot(a_ref[...], w_ref[...], preferred_element_type=F32)


def matmul_residual(a, w, res, tm=512, tn=1024):
    t, f = a.shape
    n = w.shape[1]
    tm, tn = min(tm, t), min(tn, n)
    return pl.pallas_call(
        _matmul_res_body,
        grid=(n // tn, t // tm),
        in_specs=[pl.BlockSpec((tm, f), lambda j, i: (i, 0)),
                  pl.BlockSpec((f, tn), lambda j, i: (0, j)),
                  pl.BlockSpec((tm, tn), lambda j, i: (i, j))],
        out_specs=pl.BlockSpec((tm, tn), lambda j, i: (i, j)),
        out_shape=jax.ShapeDtypeStruct((t, n), F32),
        compiler_params=_params(("parallel", "parallel"), 52),
        name="matmul_residual",
    )(a, w, res)


def _mix_out_body(tiles_per_seq, od_ref, sa_ref, sb_ref, gd_ref, gs_ref, w_ref, x_ref, o_ref, h_ref):
    dd = od_ref.shape[1]
    half = tiles_per_seq // 2
    si = pl.program_id(0) % tiles_per_seq

    @pl.when(pl.program_id(1) == 0)
    def _():
        h_ref[:, :dd] = _rms(od_ref[...].astype(F32), gd_ref[...]).astype(BF16)

        @pl.when(si < half)
        def _():
            h_ref[:, dd:] = _rms(sa_ref[...].astype(F32), gs_ref[...]).astype(BF16)

        @pl.when(si >= half)
        def _():
            h_ref[:, dd:] = _rms(sb_ref[...].astype(F32), gs_ref[...]).astype(BF16)

    o_ref[...] = x_ref[...] + jnp.dot(h_ref[...], w_ref[...], preferred_element_type=F32)


def mix_out(o_dil, o_sb_halves, g_dil, g_sb, w_o, x, seq, tm=1024, tn=1024):
    t, dd = o_dil.shape
    sa, sb = o_sb_halves
    ds = sa.shape[1]
    n = w_o.shape[1]
    tm, tn = min(tm, seq // 2), min(tn, n)
    tps = seq // tm
    half = tps // 2
    return pl.pallas_call(
        functools.partial(_mix_out_body, tps),
        grid=(t // tm, n // tn),
        in_specs=[pl.BlockSpec((tm, dd), lambda i, j: (i, 0)),
                  pl.BlockSpec((tm, ds), lambda i, j: ((i // tps) * half + jnp.minimum(i % tps, half - 1), 0)),
                  pl.BlockSpec((tm, ds), lambda i, j: ((i // tps) * half + jnp.maximum(i % tps - half, 0), 0)),
                  pl.BlockSpec((1, dd), lambda i, j: (0, 0)),
                  pl.BlockSpec((1, ds), lambda i, j: (0, 0)),
                  pl.BlockSpec((dd + ds, tn), lambda i, j: (0, j)),
                  pl.BlockSpec((tm, tn), lambda i, j: (i, j))],
        out_specs=pl.BlockSpec((tm, tn), lambda i, j: (i, j)),
        out_shape=jax.ShapeDtypeStruct((t, n), F32),
        scratch_shapes=[pltpu.VMEM((tm, dd + ds), BF16)],
        compiler_params=_params(("parallel", "arbitrary"), 56),
        name="mix_out",
    )(o_dil, sa, sb, g_dil, g_sb, w_o, x)


def _t5_bucket(dist):
    n = np.asarray(dist, dtype=np.int64)
    max_exact = N_BUCKETS // 2
    large = max_exact + (np.log(np.maximum(n, 1) / max_exact)
                         / np.log(MAX_DISTANCE / max_exact)
                         * (N_BUCKETS - max_exact)).astype(np.int64)
    large = np.minimum(large, N_BUCKETS - 1)
    return np.where(n < max_exact, n, large).astype(np.int32)


def _dilated_bias_tables(rel_bias):
    n_heads = rel_bias.shape[1]
    period = 3 * WIN
    pad = jnp.full((n_heads, period - WIN - 1), NEG_INF, F32)
    tables = []
    for (w, d) in DILATED_BRANCHES:
        assert w // d == WIN
        bias_m = rel_bias[_t5_bucket(d * np.arange(WIN, -1, -1))].astype(F32).T
        u = jnp.concatenate([bias_m, pad], axis=1)
        shifted = jnp.tile(u, (1, WIN))[:, :WIN * (period - 1)].reshape(n_heads, WIN, period - 1)
        tables.append(shifted[:, :, :2 * WIN])
    return jnp.stack(tables, axis=1)


def _dilated_body(q_ref, k_ref, v_ref, tb_ref, o_ref, ob_ref, lb_ref):
    s = q_ref.shape[0]
    scale = HEAD_DIM ** -0.5
    col = lax.broadcasted_iota(I32, (WIN, 2 * WIN), 1)

    for bi, (_, d) in enumerate(DILATED_BRANCHES):
        nblk = s // (d * WIN)

        def block(it, bi=bi, d=d, nblk=nblk):
            r = it // nblk
            blk = it % nblk
            base = r + d * WIN * blk
            prev = jnp.maximum(base - d * WIN, r)
            q = (q_ref[pl.ds(base, WIN, stride=d), :] * scale).astype(BF16)
            kk = jnp.concatenate([k_ref[pl.ds(prev, WIN, stride=d), :],
                                  k_ref[pl.ds(base, WIN, stride=d), :]], axis=0).astype(BF16)
            vv = jnp.concatenate([v_ref[pl.ds(prev, WIN, stride=d), :],
                                  v_ref[pl.ds(base, WIN, stride=d), :]], axis=0).astype(BF16)
            sc = lax.dot_general(q, kk, (((1,), (1,)), ((), ())),
                                 preferred_element_type=F32) + tb_ref[bi]
            sc = jnp.where((col < WIN) & (blk == 0), NEG_INF, sc)
            m = jnp.max(sc, axis=-1, keepdims=True)
            p = jnp.exp(sc - m)
            l = jnp.sum(p, axis=-1, keepdims=True)
            out = jnp.dot(p.astype(BF16), vv, preferred_element_type=F32) * (1.0 / l)
            lse = m + jnp.log(l)
            ob_ref[bi, pl.ds(base, WIN, stride=d), :] = out
            lb_ref[bi, pl.ds(base, WIN, stride=d), :] = jnp.broadcast_to(lse, (WIN, LANES))

        def group(g, carry, block=block):
            for u in range(DIL_UNROLL):
                block(g * DIL_UNROLL + u)
            return carry

        lax.fori_loop(0, d * nblk // DIL_UNROLL, group, 0)

    ch = 256
    def merge(c, carry):
        rows = pl.ds(pl.multiple_of(c * ch, ch), ch)
        ls = [lb_ref[b, rows, :] for b in range(len(DILATED_BRANCHES))]
        mx = functools.reduce(jnp.maximum, ls)
        es = [jnp.exp(x - mx) for x in ls]
        inv = 1.0 / functools.reduce(jnp.add, es)
        acc = es[0] * inv * ob_ref[0, rows, :]
        for b in range(1, len(DILATED_BRANCHES)):
            acc = acc + es[b] * inv * ob_ref[b, rows, :]
        o_ref[rows, :] = acc.astype(o_ref.dtype)
        return carry

    lax.fori_loop(0, s // ch, merge, 0)


def dilated_attention(proj, tables, n_heads):
    b, s, _ = proj.shape
    h = n_heads
    nbr = len(DILATED_BRANCHES)
    return pl.pallas_call(
        _dilated_body,
        grid=(b, h),
        in_specs=[pl.BlockSpec((None, s, HEAD_DIM), lambda bi, hi: (bi, 0, hi)),
                  pl.BlockSpec((None, s, HEAD_DIM), lambda bi, hi: (bi, 0, h + hi)),
                  pl.BlockSpec((None, s, HEAD_DIM), lambda bi, hi: (bi, 0, 2 * h + hi)),
                  pl.BlockSpec((None, nbr, WIN, 2 * WIN), lambda bi, hi: (hi, 0, 0, 0))],
        out_specs=pl.BlockSpec((None, s, HEAD_DIM), lambda bi, hi: (bi, 0, hi)),
        out_shape=jax.ShapeDtypeStruct((b, s, h * HEAD_DIM), BF16),
        scratch_shapes=[pltpu.VMEM((nbr, s, HEAD_DIM), F32),
                        pltpu.VMEM((nbr, s, LANES), F32)],
        compiler_params=_params(("parallel", "parallel"), 48),
        name="dilated_attention",
    )(proj, proj, proj, tables)


SB_BLOCK = 256
LOG2E = 1.4426950408889634


def _softplus2(z2):
    return jnp.maximum(z2, 0.0) + jnp.log2(1.0 + jnp.exp2(-jnp.abs(z2)))


def _sb_body(nq, qa_ref, qb_ref, k_ref, v_ref, oa_ref, ob_ref, q_ref, acc_ref, c_ref):
    blk = SB_BLOCK
    p = pl.program_id(2)
    row = lax.broadcasted_iota(I32, (blk, blk), 0)
    col = lax.broadcasted_iota(I32, (blk, blk), 1)
    tri = (row >= col).astype(BF16)
    causal = col < row
    for slot, src in enumerate((qa_ref, qb_ref)):
        q_ref[slot] = (src[...].astype(F32) * (HEAD_DIM ** -0.5 * LOG2E)).astype(BF16)

    work = [(0, p, True), (1, nq - 1 - p, True)]
    for t in range(nq - 1):
        first = t < p
        work.append((jnp.where(first, 0, 1), jnp.where(first, p - 1 - t, nq - 2 - t), False))
    zs, vs, sums, cs = {}, {}, {}, {}

    def stage_scores(i):
        slot, kb, _ = work[i]
        rows = pl.ds(pl.multiple_of(kb * blk, blk), blk)
        zs[i] = lax.dot_general(q_ref[slot], k_ref[rows, :], (((1,), (1,)), ((), ())),
                                preferred_element_type=F32)
        vs[i] = v_ref[rows, :]

    def stage_sums(i):
        slot, _, diag = work[i]
        sp = _softplus2(zs[i])
        if diag:
            sp = jnp.where(causal, sp, 0.0)
        total = jnp.broadcast_to(jnp.sum(sp, axis=-1, keepdims=True), (blk, LANES))
        if diag:
            c_ref[slot] = total
        else:
            cs[i] = c_ref[slot]
            c_ref[slot] = cs[i] + total
        sums[i] = jnp.dot(sp.astype(BF16), tri, preferred_element_type=F32)

    def stage_out(i):
        slot, _, diag = work[i]
        e = zs.pop(i) - sums.pop(i)
        if diag:
            a = jnp.where(causal, jnp.exp2(e), 0.0)
        else:
            c = cs.pop(i)
            a = jnp.exp2(e - jnp.concatenate([c, c], axis=1))
        pv = jnp.dot(a.astype(BF16), vs.pop(i), preferred_element_type=F32)
        if diag:
            acc_ref[slot] = pv
        else:
            acc_ref[slot] += pv

    n = len(work)
    for step in range(n + 2):
        if step < n:
            stage_scores(step)
        if 0 <= step - 1 < n:
            stage_sums(step - 1)
        if 0 <= step - 2 < n:
            stage_out(step - 2)
    oa_ref[...] = acc_ref[0].astype(oa_ref.dtype)
    ob_ref[...] = acc_ref[1].astype(ob_ref.dtype)


def stick_breaking_attention(proj, n_heads):
    b, s, _ = proj.shape
    h = n_heads
    blk = SB_BLOCK
    nq = s // blk
    assert LANES * 2 == blk and nq % 2 == 0
    q_spec = lambda f: pl.BlockSpec((None, blk, HEAD_DIM), lambda bi, hi, p: (bi, f(p), hi))
    oa, ob = pl.pallas_call(
        functools.partial(_sb_body, nq),
        grid=(b, h, nq // 2),
        in_specs=[q_spec(lambda p: p), q_spec(lambda p: nq - 1 - p),
                  pl.BlockSpec((None, s, HEAD_DIM), lambda bi, hi, p: (bi, 0, h + hi)),
                  pl.BlockSpec((None, s, HEAD_DIM), lambda bi, hi, p: (bi, 0, 2 * h + hi))],
        out_specs=[q_spec(lambda p: p), q_spec(lambda p: nq // 2 - 1 - p)],
        out_shape=[jax.ShapeDtypeStruct((b, s // 2, h * HEAD_DIM), BF16)] * 2,
        scratch_shapes=[pltpu.VMEM((2, blk, HEAD_DIM), BF16),
                        pltpu.VMEM((2, blk, HEAD_DIM), F32),
                        pltpu.VMEM((2, blk, LANES), F32)],
        compiler_params=_params(("parallel", "parallel", "parallel"), 32),
        name="stick_breaking_attention",
    )(proj, proj, proj, proj)
    return oa, ob


MOE_TILE = 512
DMA_ISSUE_UNROLL = 8
META_I1, META_I2, META_W1, META_W2, META_P1, META_P2 = range(6)


def _router_body(n_experts, x_ref, g_ref, rw_ref, h_ref, meta_ref, cnt_ref, carry_ref):
    tm = x_ref.shape[0]

    @pl.when(pl.program_id(0) == 0)
    def _():
        carry_ref[...] = jnp.zeros_like(carry_ref)

    h = _rms(x_ref[...], g_ref[...])
    h_ref[...] = h
    logits = jnp.dot(h, rw_ref[...], precision=lax.Precision.HIGHEST,
                     preferred_element_type=F32)
    lane = lax.broadcasted_iota(I32, logits.shape, 1)
    l1 = jnp.where(lane < n_experts, logits, NEG_INF)
    m1 = jnp.max(l1, axis=-1, keepdims=True)
    i1 = jnp.min(jnp.where(l1 == m1, lane, LANES), axis=-1, keepdims=True)
    l2 = jnp.where(lane == i1, NEG_INF, l1)
    m2 = jnp.max(l2, axis=-1, keepdims=True)
    i2 = jnp.min(jnp.where(l2 == m2, lane, LANES), axis=-1, keepdims=True)
    e2 = jnp.exp(m2 - m1)
    inv = 1.0 / (1.0 + e2)
    w1, w2 = inv, e2 * inv
    sel1, sel2 = lane == i1, lane == i2
    onehot = jnp.where(sel1 | sel2, 1.0, 0.0)
    r = lax.broadcasted_iota(I32, (tm, tm), 0)
    c = lax.broadcasted_iota(I32, (tm, tm), 1)
    before = (c < r).astype(BF16)
    pos = jnp.dot(before, onehot.astype(BF16), preferred_element_type=F32) + carry_ref[...]
    p1 = jnp.sum(jnp.where(sel1, pos, 0.0), axis=-1, keepdims=True)
    p2 = jnp.sum(jnp.where(sel2, pos, 0.0), axis=-1, keepdims=True)
    meta = jnp.zeros_like(logits)
    for idx, val in ((META_I1, i1.astype(F32)), (META_I2, i2.astype(F32)), (META_W1, w1),
                     (META_W2, w2), (META_P1, p1), (META_P2, p2)):
        meta = jnp.where(lane == idx, val, meta)
    meta_ref[...] = meta
    carry_ref[...] += jnp.sum(onehot, axis=0, keepdims=True)
    cnt_ref[...] = carry_ref[...]


def moe_router(x, g, rw, n_experts, tm=512):
    t, d = x.shape
    tm = min(tm, t)
    return pl.pallas_call(
        functools.partial(_router_body, n_experts),
        grid=(t // tm,),
        in_specs=[pl.BlockSpec((tm, d), lambda i: (i, 0)),
                  pl.BlockSpec((1, d), lambda i: (0, 0)),
                  pl.BlockSpec((d, LANES), lambda i: (0, 0))],
        out_specs=[pl.BlockSpec((tm, d), lambda i: (i, 0)),
                   pl.BlockSpec((tm, LANES), lambda i: (i, 0)),
                   pl.BlockSpec((1, LANES), lambda i: (0, 0))],
        out_shape=[jax.ShapeDtypeStruct((t, d), F32),
                   jax.ShapeDtypeStruct((t, LANES), F32),
                   jax.ShapeDtypeStruct((1, LANES), F32)],
        scratch_shapes=[pltpu.VMEM((1, LANES), F32)],
        compiler_params=_params(("arbitrary",), 32),
        name="moe_router",
    )(x, g, rw)


def _row_copy(src_hbm, src_row, dst_ref, dst_row, sem):
    return pltpu.make_async_copy(src_hbm.at[pl.ds(src_row, 1), :],
                                 dst_ref.at[pl.ds(dst_row, 1), :], sem)


def _gather_body(tok_ref, h_hbm, o_ref, sem):
    tg = o_ref.shape[0]
    base = pl.program_id(0) * tg

    def issue(r, carry):
        _row_copy(h_hbm, tok_ref[base + r], o_ref, r, sem).start()
        return carry

    lax.fori_loop(0, tg, issue, 0, unroll=DMA_ISSUE_UNROLL)
    pltpu.make_async_copy(h_hbm.at[pl.ds(0, tg), :], o_ref, sem).wait()


def gather_rows(h, tok, tg=256):
    t, d = h.shape
    rows = tok.shape[0]
    return pl.pallas_call(
        _gather_body,
        grid_spec=pltpu.PrefetchScalarGridSpec(
            num_scalar_prefetch=1,
            grid=(rows // tg,),
            in_specs=[pl.BlockSpec(memory_space=pl.ANY)],
            out_specs=pl.BlockSpec((tg, d), lambda i, tok: (i, 0)),
            scratch_shapes=[pltpu.SemaphoreType.DMA(())]),
        out_shape=jax.ShapeDtypeStruct((rows, d), h.dtype),
        compiler_params=_params(("arbitrary",), 32),
        name="moe_gather",
    )(tok, h)


def _moe_up_body(te_ref, nv_ref, x_ref, wg_ref, wu_ref, o_ref, h_ref):
    i, j = pl.program_id(0), pl.program_id(1)

    @pl.when(i < nv_ref[0])
    def _():
        @pl.when(j == 0)
        def _():
            h_ref[...] = x_ref[...].astype(BF16)

        h = h_ref[...]
        a = jnp.dot(h, wg_ref[...], preferred_element_type=F32)
        b = jnp.dot(h, wu_ref[...], preferred_element_type=F32)
        o_ref[...] = _silu_mul(a, b).astype(o_ref.dtype)

    @pl.when(i >= nv_ref[0])
    def _():
        o_ref[...] = jnp.zeros_like(o_ref)


def moe_up(xs, tile_expert, n_valid, wg, wu, tn=1024):
    rows, d = xs.shape
    f = wg.shape[2]
    tm, tn = MOE_TILE, min(tn, f)
    return pl.pallas_call(
        _moe_up_body,
        grid_spec=pltpu.PrefetchScalarGridSpec(
            num_scalar_prefetch=2,
            grid=(rows // tm, f // tn),
            in_specs=[pl.BlockSpec((tm, d), lambda i, j, te, nv: (i, 0)),
                      pl.BlockSpec((None, d, tn), lambda i, j, te, nv: (te[i], 0, j)),
                      pl.BlockSpec((None, d, tn), lambda i, j, te, nv: (te[i], 0, j))],
            out_specs=pl.BlockSpec((tm, tn), lambda i, j, te, nv: (i, j)),
            scratch_shapes=[pltpu.VMEM((tm, d), BF16)]),
        out_shape=jax.ShapeDtypeStruct((rows, f), BF16),
        compiler_params=_params(("parallel", "arbitrary"), 48),
        name="moe_up",
    )(tile_expert, n_valid, xs, wg, wu)


def _moe_down_body(te_ref, nv_ref, a_ref, w_ref, o_ref):
    i = pl.program_id(0)

    @pl.when(i < nv_ref[0])
    def _():
        o_ref[...] = jnp.dot(a_ref[...], w_ref[...], preferred_element_type=F32)

    @pl.when(i >= nv_ref[0])
    def _():
        o_ref[...] = jnp.zeros_like(o_ref)


def moe_down(act, tile_expert, n_valid, wd, tn=512):
    rows, f = act.shape
    d = wd.shape[2]
    tm, tn = MOE_TILE, min(tn, d)
    return pl.pallas_call(
        _moe_down_body,
        grid_spec=pltpu.PrefetchScalarGridSpec(
            num_scalar_prefetch=2,
            grid=(rows // tm, d // tn),
            in_specs=[pl.BlockSpec((tm, f), lambda i, j, te, nv: (i, 0)),
                      pl.BlockSpec((None, f, tn), lambda i, j, te, nv: (te[i], 0, j))],
            out_specs=pl.BlockSpec((tm, tn), lambda i, j, te, nv: (i, j))),
        out_shape=jax.ShapeDtypeStruct((rows, d), F32),
        compiler_params=_params(("parallel", "arbitrary"), 56),
        name="moe_down",
    )(tile_expert, n_valid, act, wd)


def _combine_body(d1_ref, d2_ref, x_ref, meta_ref, g_ref, y_hbm, o_ref, buf_ref, sem):
    tc = x_ref.shape[0]
    base = pl.program_id(0) * tc

    def issue(r, carry):
        _row_copy(y_hbm, d1_ref[base + r], buf_ref.at[0], r, sem).start()
        _row_copy(y_hbm, d2_ref[base + r], buf_ref.at[1], r, sem).start()
        return carry

    lax.fori_loop(0, tc, issue, 0, unroll=DMA_ISSUE_UNROLL)
    for k in range(TOP_K):
        pltpu.make_async_copy(y_hbm.at[pl.ds(0, tc), :], buf_ref.at[k], sem).wait()
    meta = meta_ref[...]
    w1 = meta[:, META_W1:META_W1 + 1]
    w2 = meta[:, META_W2:META_W2 + 1]
    y = x_ref[...] + (w1 * buf_ref[0] + w2 * buf_ref[1])
    o_ref[...] = _rms(y, g_ref[...])


def moe_combine_norm(x, meta, d1, d2, y, g, tc=256):
    t, d = x.shape
    tc = min(tc, t)
    return pl.pallas_call(
        _combine_body,
        grid_spec=pltpu.PrefetchScalarGridSpec(
            num_scalar_prefetch=2,
            grid=(t // tc,),
            in_specs=[pl.BlockSpec((tc, d), lambda i, a, b: (i, 0)),
                      pl.BlockSpec((tc, LANES), lambda i, a, b: (i, 0)),
                      pl.BlockSpec((1, d), lambda i, a, b: (0, 0)),
                      pl.BlockSpec(memory_space=pl.ANY)],
            out_specs=pl.BlockSpec((tc, d), lambda i, a, b: (i, 0)),
            scratch_shapes=[pltpu.VMEM((2, tc, d), F32),
                            pltpu.SemaphoreType.DMA(())]),
        out_shape=jax.ShapeDtypeStruct((t, d), F32),
        compiler_params=_params(("arbitrary",), 32),
        name="moe_combine_norm",
    )(d1, d2, x, meta, g, y)


def moe_layer_and_final_norm(x, g, router_w, w_gate, w_up, w_down, final_g):
    t, d = x.shape
    n_experts = router_w.shape[1]
    tm = MOE_TILE
    rw = jnp.zeros((d, LANES), F32).at[:, :n_experts].set(router_w)
    h, meta, cnt = moe_router(x, g, rw, n_experts)

    rows = t * TOP_K + n_experts * tm
    n_tiles = rows // tm
    counts = cnt[0, :n_experts].astype(I32)
    tiles_per = (counts + tm - 1) // tm
    tile_end = jnp.cumsum(tiles_per)
    offs = (tile_end - tiles_per) * tm
    n_valid = tile_end[-1:]
    tile_expert = jnp.minimum(
        jnp.searchsorted(tile_end, jnp.arange(n_tiles, dtype=I32), side="right"),
        n_experts - 1).astype(I32)
    i1 = meta[:, META_I1].astype(I32)
    i2 = meta[:, META_I2].astype(I32)
    d1 = offs[i1] + meta[:, META_P1].astype(I32)
    d2 = offs[i2] + meta[:, META_P2].astype(I32)
    tid = jnp.arange(t, dtype=I32)
    tok = jnp.zeros((rows,), I32).at[jnp.concatenate([d1, d2])].set(jnp.concatenate([tid, tid]))

    xs = gather_rows(h, tok)
    act = moe_up(xs, tile_expert, n_valid, w_gate, w_up)
    y = moe_down(act, tile_expert, n_valid, w_down)
    return moe_combine_norm(x, meta, d1, d2, y, final_g)


def kernel(x, rel_bias, attn_norm_g, w_in, mix_norm_dil_g, mix_norm_sb_g, w_o, ffn_norm_g,
           dense_w_gate, dense_w_up, dense_w_down, router_w, moe_w_gate, moe_w_up,
           moe_w_down, final_norm_g):
    b, s, d = x.shape
    depth = w_in.shape[0]
    assert depth == 2, "layer 0 dense SwiGLU, layer 1 routed experts followed by the final norm"
    d_dil = mix_norm_dil_g.shape[1]
    d_sb = mix_norm_sb_g.shape[1]
    h_dil, h_sb = d_dil // HEAD_DIM, d_sb // HEAD_DIM
    tables = _dilated_bias_tables(rel_bias)
    row = lambda v: v.reshape(1, -1)

    xt = x.reshape(b * s, d)
    for layer in range(depth):
        w_in_l = w_in[layer].astype(BF16)
        g_attn = row(attn_norm_g[layer])
        p_dil, p_sb = norm_matmul_split(xt, g_attn, w_in_l, 3 * d_dil, F32, BF16)
        o_dil = dilated_attention(p_dil.reshape(b, s, 3 * d_dil), tables, h_dil)
        o_sb = stick_breaking_attention(p_sb.reshape(b, s, 3 * d_sb), h_sb)
        xt = mix_out(o_dil.reshape(b * s, d_dil),
                     tuple(o.reshape(b * s // 2, d_sb) for o in o_sb),
                     row(mix_norm_dil_g[layer]), row(mix_norm_sb_g[layer]),
                     w_o[layer].astype(BF16), xt, s)
        g_ffn = row(ffn_norm_g[layer])
        j = layer // 2
        if layer % 2 == 0:
            act = ffn_up(xt, g_ffn, dense_w_gate[j].astype(BF16), dense_w_up[j].astype(BF16))
            xt = matmul_residual(act, dense_w_down[j].astype(BF16), xt)
        else:
            xt = moe_layer_and_final_norm(
                xt, g_ffn, router_w[j], moe_w_gate[j].astype(BF16), moe_w_up[j].astype(BF16),
                moe_w_down[j].astype(BF16), row(final_norm_g))
    return xt.reshape(b, s, d)
```

```python
import functools

import numpy as np
import jax
import jax.numpy as jnp
from jax import lax
from jax.experimental import pallas as pl
from jax.experimental.pallas import tpu as pltpu

F32 = jnp.float32
BF16 = jnp.bfloat16
I32 = jnp.int32

EPS = 1e-6
HEAD_DIM = 128
DILATED_BRANCHES = ((128, 1), (512, 4), (2048, 16))
N_BUCKETS = 32
MAX_DISTANCE = 2048
TOP_K = 2
LANES = 128
WIN = 128
NEG_INF = float("-inf")
DIL_UNROLL = 8
MIB = 1024 * 1024


def _params(semantics, vmem_mib):
    return pltpu.CompilerParams(dimension_semantics=semantics,
                                vmem_limit_bytes=vmem_mib * MIB)


def _rms(x, g):
    return x * lax.rsqrt(jnp.mean(x * x, axis=-1, keepdims=True) + EPS) * g


def _norm_matmul_body(n_first, x_ref, g_ref, w_ref, oa_ref, ob_ref, h_ref):
    j = pl.program_id(1)

    @pl.when(j == 0)
    def _():
        h_ref[...] = _rms(x_ref[...], g_ref[...]).astype(BF16)

    y = jnp.dot(h_ref[...], w_ref[...], preferred_element_type=F32)

    @pl.when(j < n_first)
    def _():
        oa_ref[...] = y.astype(oa_ref.dtype)

    @pl.when(j >= n_first)
    def _():
        ob_ref[...] = y.astype(ob_ref.dtype)


def norm_matmul_split(x, g, w, n_a, dtype_a, dtype_b, tm=1024, tn=1024):
    t, d = x.shape
    n = w.shape[1]
    tm, tn = min(tm, t), min(tn, n_a, n - n_a)
    assert n_a % tn == 0 and (n - n_a) % tn == 0
    ja = n_a // tn
    return pl.pallas_call(
        functools.partial(_norm_matmul_body, ja),
        grid=(t // tm, n // tn),
        in_specs=[pl.BlockSpec((tm, d), lambda i, j: (i, 0)),
                  pl.BlockSpec((1, d), lambda i, j: (0, 0)),
                  pl.BlockSpec((d, tn), lambda i, j: (0, j))],
        out_specs=[pl.BlockSpec((tm, tn), lambda i, j: (i, jnp.minimum(j, ja - 1))),
                   pl.BlockSpec((tm, tn), lambda i, j: (i, jnp.maximum(j - ja, 0)))],
        out_shape=[jax.ShapeDtypeStruct((t, n_a), dtype_a),
                   jax.ShapeDtypeStruct((t, n - n_a), dtype_b)],
        scratch_shapes=[pltpu.VMEM((tm, d), BF16)],
        compiler_params=_params(("parallel", "arbitrary"), 56),
        name="norm_matmul",
    )(x, g, w)


def _silu_mul(a, b):
    return a * (1.0 / (1.0 + jnp.exp(-a))) * b


def _ffn_up_body(x_ref, g_ref, wg_ref, wu_ref, o_ref, h_ref):
    @pl.when(pl.program_id(1) == 0)
    def _():
        h_ref[...] = _rms(x_ref[...], g_ref[...]).astype(BF16)

    h = h_ref[...]
    a = jnp.dot(h, wg_ref[...], preferred_element_type=F32)
    b = jnp.dot(h, wu_ref[...], preferred_element_type=F32)
    o_ref[...] = _silu_mul(a, b).astype(o_ref.dtype)


def ffn_up(x, g, wg, wu, tm=1024, tn=512):
    t, d = x.shape
    f = wg.shape[1]
    tm, tn = min(tm, t), min(tn, f)
    return pl.pallas_call(
        _ffn_up_body,
        grid=(t // tm, pl.cdiv(f, tn)),
        in_specs=[pl.BlockSpec((tm, d), lambda i, j: (i, 0)),
                  pl.BlockSpec((1, d), lambda i, j: (0, 0)),
                  pl.BlockSpec((d, tn), lambda i, j: (0, j)),
                  pl.BlockSpec((d, tn), lambda i, j: (0, j))],
        out_specs=pl.BlockSpec((tm, tn), lambda i, j: (i, j)),
        out_shape=jax.ShapeDtypeStruct((t, f), BF16),
        scratch_shapes=[pltpu.VMEM((tm, d), BF16)],
        compiler_params=_params(("parallel", "arbitrary"), 48),
        name="ffn_up",
    )(x, g, wg, wu)


def _matmul_res_body(a_ref, w_ref, r_ref, o_ref):
    o_ref[...] = r_ref[...] + jnp.dot(a_ref[...], w_ref[...], preferred_element_type=F32)


def matmul_residual(a, w, res, tm=512, tn=1024):
    t, f = a.shape
    n = w.shape[1]
    tm, tn = min(tm, t), min(tn, n)
    return pl.pallas_call(
        _matmul_res_body,
        grid=(n // tn, t // tm),
        in_specs=[pl.BlockSpec((tm, f), lambda j, i: (i, 0)),
                  pl.BlockSpec((f, tn), lambda j, i: (0, j)),
                  pl.BlockSpec((tm, tn), lambda j, i: (i, j))],
        out_specs=pl.BlockSpec((tm, tn), lambda j, i: (i, j)),
        out_shape=jax.ShapeDtypeStruct((t, n), F32),
        compiler_params=_params(("parallel", "parallel"), 52),
        name="matmul_residual",
    )(a, w, res)


def _mix_out_body(tiles_per_seq, od_ref, sa_ref, sb_ref, gd_ref, gs_ref, w_ref, x_ref, o_ref, h_ref):
    dd = od_ref.shape[1]
    half = tiles_per_seq // 2
    si = pl.program_id(0) % tiles_per_seq

    @pl.when(pl.program_id(1) == 0)
    def _():
        h_ref[:, :dd] = _rms(od_ref[...].astype(F32), gd_ref[...]).astype(BF16)

        @pl.when(si < half)
        def _():
            h_ref[:, dd:] = _rms(sa_ref[...].astype(F32), gs_ref[...]).astype(BF16)

        @pl.when(si >= half)
        def _():
            h_ref[:, dd:] = _rms(sb_ref[...].astype(F32), gs_ref[...]).astype(BF16)

    o_ref[...] = x_ref[...] + jnp.dot(h_ref[...], w_ref[...], preferred_element_type=F32)


def mix_out(o_dil, o_sb_halves, g_dil, g_sb, w_o, x, seq, tm=1024, tn=1024):
    t, dd = o_dil.shape
    sa, sb = o_sb_halves
    ds = sa.shape[1]
    n = w_o.shape[1]
    tm, tn = min(tm, seq // 2), min(tn, n)
    tps = seq // tm
    half = tps // 2
    return pl.pallas_call(
        functools.partial(_mix_out_body, tps),
        grid=(t // tm, n // tn),
        in_specs=[pl.BlockSpec((tm, dd), lambda i, j: (i, 0)),
                  pl.BlockSpec((tm, ds), lambda i, j: ((i // tps) * half + jnp.minimum(i % tps, half - 1), 0)),
                  pl.BlockSpec((tm, ds), lambda i, j: ((i // tps) * half + jnp.maximum(i % tps - half, 0), 0)),
                  pl.BlockSpec((1, dd), lambda i, j: (0, 0)),
                  pl.BlockSpec((1, ds), lambda i, j: (0, 0)),
                  pl.BlockSpec((dd + ds, tn), lambda i, j: (0, j)),
                  pl.BlockSpec((tm, tn), lambda i, j: (i, j))],
        out_specs=pl.BlockSpec((tm, tn), lambda i, j: (i, j)),
        out_shape=jax.ShapeDtypeStruct((t, n), F32),
        scratch_shapes=[pltpu.VMEM((tm, dd + ds), BF16)],
        compiler_params=_params(("parallel", "arbitrary"), 56),
        name="mix_out",
    )(o_dil, sa, sb, g_dil, g_sb, w_o, x)


def _t5_bucket(dist):
    n = np.asarray(dist, dtype=np.int64)
    max_exact = N_BUCKETS // 2
    large = max_exact + (np.log(np.maximum(n, 1) / max_exact)
                         / np.log(MAX_DISTANCE / max_exact)
                         * (N_BUCKETS - max_exact)).astype(np.int64)
    large = np.minimum(large, N_BUCKETS - 1)
    return np.where(n < max_exact, n, large).astype(np.int32)


def _dilated_bias_tables(rel_bias):
    n_heads = rel_bias.shape[1]
    period = 3 * WIN
    pad = jnp.full((n_heads, period - WIN - 1), NEG_INF, F32)
    tables = []
    for (w, d) in DILATED_BRANCHES:
        assert w // d == WIN
        bias_m = rel_bias[_t5_bucket(d * np.arange(WIN, -1, -1))].astype(F32).T
        u = jnp.concatenate([bias_m, pad], axis=1)
        shifted = jnp.tile(u, (1, WIN))[:, :WIN * (period - 1)].reshape(n_heads, WIN, period - 1)
        tables.append(shifted[:, :, :2 * WIN])
    return jnp.stack(tables, axis=1)


def _dilated_body(q_ref, k_ref, v_ref, tb_ref, o_ref, ob_ref, lb_ref):
    s = q_ref.shape[0]
    scale = HEAD_DIM ** -0.5

    def load(ref, pos, d):
        return ref[pl.ds(pos, WIN, stride=d), :]

    def attend(bi, d, base, q, kk, vv, tb):
        sc = lax.dot_general(q, kk, (((1,), (1,)), ((), ())), preferred_element_type=F32) + tb
        m = jnp.max(sc, axis=-1, keepdims=True)
        p = jnp.exp(sc - m)
        l = jnp.sum(p, axis=-1, keepdims=True)
        out = jnp.dot(p.astype(BF16), vv, preferred_element_type=F32) * (1.0 / l)
        lse = m + jnp.log(l)
        ob_ref[bi, pl.ds(base, WIN, stride=d), :] = out
        lb_ref[bi, pl.ds(base, WIN, stride=d), :] = jnp.broadcast_to(lse, (WIN, LANES))

    def run(bi, d, start, nb, k_prev, v_prev):
        for b in range(nb):
            base = start + d * WIN * b
            q = (load(q_ref, base, d) * scale).astype(BF16)
            k_cur = load(k_ref, base, d).astype(BF16)
            v_cur = load(v_ref, base, d).astype(BF16)
            if k_prev is None:
                attend(bi, d, base, q, k_cur, v_cur, tb_ref[bi, :, WIN:])
            else:
                attend(bi, d, base, q, jnp.concatenate([k_prev, k_cur], axis=0),
                       jnp.concatenate([v_prev, v_cur], axis=0), tb_ref[bi])
            k_prev, v_prev = k_cur, v_cur

    for bi, (_, d) in enumerate(DILATED_BRANCHES):
        nblk = s // (d * WIN)
        if nblk <= DIL_UNROLL:
            per = DIL_UNROLL // nblk

            def trip(g, carry, bi=bi, d=d, nblk=nblk, per=per):
                for u in range(per):
                    run(bi, d, g * per + u, nblk, None, None)
                return carry

            lax.fori_loop(0, d // per, trip, 0)
        else:
            chunks = nblk // DIL_UNROLL
            span = d * WIN * DIL_UNROLL

            def head(r, carry, bi=bi, d=d):
                run(bi, d, r, DIL_UNROLL, None, None)
                return carry

            def tail(it, carry, bi=bi, d=d, chunks=chunks, span=span):
                start = it // (chunks - 1) + span * (it % (chunks - 1) + 1)
                run(bi, d, start, DIL_UNROLL, load(k_ref, start - d * WIN, d).astype(BF16),
                    load(v_ref, start - d * WIN, d).astype(BF16))
                return carry

            lax.fori_loop(0, d, head, 0)
            lax.fori_loop(0, d * (chunks - 1), tail, 0)

    ch = 256
    def merge(c, carry):
        rows = pl.ds(pl.multiple_of(c * ch, ch), ch)
        ls = [lb_ref[b, rows, :] for b in range(len(DILATED_BRANCHES))]
        mx = functools.reduce(jnp.maximum, ls)
        es = [jnp.exp(x - mx) for x in ls]
        inv = 1.0 / functools.reduce(jnp.add, es)
        acc = es[0] * inv * ob_ref[0, rows, :]
        for b in range(1, len(DILATED_BRANCHES)):
            acc = acc + es[b] * inv * ob_ref[b, rows, :]
        o_ref[rows, :] = acc.astype(o_ref.dtype)
        return carry

    lax.fori_loop(0, s // ch, merge, 0)


def dilated_attention(proj, tables, n_heads):
    b, s, _ = proj.shape
    h = n_heads
    nbr = len(DILATED_BRANCHES)
    return pl.pallas_call(
        _dilated_body,
        grid=(b, h),
        in_specs=[pl.BlockSpec((None, s, HEAD_DIM), lambda bi, hi: (bi, 0, hi)),
                  pl.BlockSpec((None, s, HEAD_DIM), lambda bi, hi: (bi, 0, h + hi)),
                  pl.BlockSpec((None, s, HEAD_DIM), lambda bi, hi: (bi, 0, 2 * h + hi)),
                  pl.BlockSpec((None, nbr, WIN, 2 * WIN), lambda bi, hi: (hi, 0, 0, 0))],
        out_specs=pl.BlockSpec((None, s, HEAD_DIM), lambda bi, hi: (bi, 0, hi)),
        out_shape=jax.ShapeDtypeStruct((b, s, h * HEAD_DIM), BF16),
        scratch_shapes=[pltpu.VMEM((nbr, s, HEAD_DIM), F32),
                        pltpu.VMEM((nbr, s, LANES), F32)],
        compiler_params=_params(("parallel", "parallel"), 48),
        name="dilated_attention",
    )(proj, proj, proj, tables)


SB_BLOCK = 256
LOG2E = 1.4426950408889634


def _softplus2(z2):
    return jnp.maximum(z2, 0.0) + jnp.log2(1.0 + jnp.exp2(-jnp.abs(z2)))


def _sb_body(nq, qa_ref, qb_ref, k_ref, v_ref, oa_ref, ob_ref, q_ref, acc_ref, c_ref):
    blk = SB_BLOCK
    p = pl.program_id(2)
    row = lax.broadcasted_iota(I32, (blk, blk), 0)
    col = lax.broadcasted_iota(I32, (blk, blk), 1)
    tri = (row >= col).astype(BF16)
    causal = col < row
    for slot, src in enumerate((qa_ref, qb_ref)):
        q_ref[slot] = (src[...].astype(F32) * (HEAD_DIM ** -0.5 * LOG2E)).astype(BF16)

    work = [(0, p, True), (1, nq - 1 - p, True)]
    for t in range(nq - 1):
        first = t < p
        work.append((jnp.where(first, 0, 1), jnp.where(first, p - 1 - t, nq - 2 - t), False))
    zs, vs, sums, cs = {}, {}, {}, {}

    def stage_scores(i):
        slot, kb, _ = work[i]
        rows = pl.ds(pl.multiple_of(kb * blk, blk), blk)
        zs[i] = lax.dot_general(q_ref[slot], k_ref[rows, :], (((1,), (1,)), ((), ())),
                                preferred_element_type=F32)
        vs[i] = v_ref[rows, :]

    def stage_sums(i):
        slot, _, diag = work[i]
        sp = _softplus2(zs[i])
        if diag:
            sp = jnp.where(causal, sp, 0.0)
        total = jnp.broadcast_to(jnp.sum(sp, axis=-1, keepdims=True), (blk, LANES))
        if diag:
            c_ref[slot] = total
        else:
            cs[i] = c_ref[slot]
            c_ref[slot] = cs[i] + total
        sums[i] = jnp.dot(sp.astype(BF16), tri, preferred_element_type=F32)

    def stage_out(i):
        slot, _, diag = work[i]
        e = zs.pop(i) - sums.pop(i)
        if diag:
            a = jnp.where(causal, jnp.exp2(e), 0.0)
        else:
            c = cs.pop(i)
            a = jnp.exp2(e - jnp.concatenate([c, c], axis=1))
        pv = jnp.dot(a.astype(BF16), vs.pop(i), preferred_element_type=F32)
        if diag:
            acc_ref[slot] = pv
        else:
            acc_ref[slot] += pv

    n = len(work)
    for step in range(n + 2):
        if step < n:
            stage_scores(step)
        if 0 <= step - 1 < n:
            stage_sums(step - 1)
        if 0 <= step - 2 < n:
            stage_out(step - 2)
    oa_ref[...] = acc_ref[0].astype(oa_ref.dtype)
    ob_ref[...] = acc_ref[1].astype(ob_ref.dtype)


def stick_breaking_attention(proj, n_heads):
    b, s, _ = proj.shape
    h = n_heads
    blk = SB_BLOCK
    nq = s // blk
    assert LANES * 2 == blk and nq % 2 == 0
    q_spec = lambda f: pl.BlockSpec((None, blk, HEAD_DIM), lambda bi, hi, p: (bi, f(p), hi))
    oa, ob = pl.pallas_call(
        functools.partial(_sb_body, nq),
        grid=(b, h, nq // 2),
        in_specs=[q_spec(lambda p: p), q_spec(lambda p: nq - 1 - p),
                  pl.BlockSpec((None, s, HEAD_DIM), lambda bi, hi, p: (bi, 0, h + hi)),
                  pl.BlockSpec((None, s, HEAD_DIM), lambda bi, hi, p: (bi, 0, 2 * h + hi))],
        out_specs=[q_spec(lambda p: p), q_spec(lambda p: nq // 2 - 1 - p)],
        out_shape=[jax.ShapeDtypeStruct((b, s // 2, h * HEAD_DIM), BF16)] * 2,
        scratch_shapes=[pltpu.VMEM((2, blk, HEAD_DIM), BF16),
                        pltpu.VMEM((2, blk, HEAD_DIM), F32),
                        pltpu.VMEM((2, blk, LANES), F32)],
        compiler_params=_params(("parallel", "parallel", "parallel"), 32),
        name="stick_breaking_attention",
    )(proj, proj, proj, proj)
    return oa, ob


MOE_TILE = 512
DMA_ISSUE_UNROLL = 8
META_I1, META_I2, META_W1, META_W2, META_P1, META_P2 = range(6)


def _router_body(n_experts, x_ref, g_ref, rw_ref, h_ref, meta_ref, cnt_ref, carry_ref):
    tm = x_ref.shape[0]

    @pl.when(pl.program_id(0) == 0)
    def _():
        carry_ref[...] = jnp.zeros_like(carry_ref)

    h = _rms(x_ref[...], g_ref[...])
    h_ref[...] = h
    logits = jnp.dot(h, rw_ref[...], precision=lax.Precision.HIGHEST,
                     preferred_element_type=F32)
    lane = lax.broadcasted_iota(I32, logits.shape, 1)
    l1 = jnp.where(lane < n_experts, logits, NEG_INF)
    m1 = jnp.max(l1, axis=-1, keepdims=True)
    i1 = jnp.min(jnp.where(l1 == m1, lane, LANES), axis=-1, keepdims=True)
    l2 = jnp.where(lane == i1, NEG_INF, l1)
    m2 = jnp.max(l2, axis=-1, keepdims=True)
    i2 = jnp.min(jnp.where(l2 == m2, lane, LANES), axis=-1, keepdims=True)
    e2 = jnp.exp(m2 - m1)
    inv = 1.0 / (1.0 + e2)
    w1, w2 = inv, e2 * inv
    sel1, sel2 = lane == i1, lane == i2
    onehot = jnp.where(sel1 | sel2, 1.0, 0.0)
    r = lax.broadcasted_iota(I32, (tm, tm), 0)
    c = lax.broadcasted_iota(I32, (tm, tm), 1)
    before = (c < r).astype(BF16)
    pos = jnp.dot(before, onehot.astype(BF16), preferred_element_type=F32) + carry_ref[...]
    p1 = jnp.sum(jnp.where(sel1, pos, 0.0), axis=-1, keepdims=True)
    p2 = jnp.sum(jnp.where(sel2, pos, 0.0), axis=-1, keepdims=True)
    meta = jnp.zeros_like(logits)
    for idx, val in ((META_I1, i1.astype(F32)), (META_I2, i2.astype(F32)), (META_W1, w1),
                     (META_W2, w2), (META_P1, p1), (META_P2, p2)):
        meta = jnp.where(lane == idx, val, meta)
    meta_ref[...] = meta
    carry_ref[...] += jnp.sum(onehot, axis=0, keepdims=True)
    cnt_ref[...] = carry_ref[...]


def moe_router(x, g, rw, n_experts, tm=512):
    t, d = x.shape
    tm = min(tm, t)
    return pl.pallas_call(
        functools.partial(_router_body, n_experts),
        grid=(t // tm,),
        in_specs=[pl.BlockSpec((tm, d), lambda i: (i, 0)),
                  pl.BlockSpec((1, d), lambda i: (0, 0)),
                  pl.BlockSpec((d, LANES), lambda i: (0, 0))],
        out_specs=[pl.BlockSpec((tm, d), lambda i: (i, 0)),
                   pl.BlockSpec((tm, LANES), lambda i: (i, 0)),
                   pl.BlockSpec((1, LANES), lambda i: (0, 0))],
        out_shape=[jax.ShapeDtypeStruct((t, d), F32),
                   jax.ShapeDtypeStruct((t, LANES), F32),
                   jax.ShapeDtypeStruct((1, LANES), F32)],
        scratch_shapes=[pltpu.VMEM((1, LANES), F32)],
        compiler_params=_params(("arbitrary",), 32),
        name="moe_router",
    )(x, g, rw)


def _row_copy(src_hbm, src_row, dst_ref, dst_row, sem):
    return pltpu.make_async_copy(src_hbm.at[pl.ds(src_row, 1), :],
                                 dst_ref.at[pl.ds(dst_row, 1), :], sem)


def _moe_up_body(te_ref, nv_ref, tok_ref, h_hbm, wg_ref, wu_ref, o_ref, x_ref, h_ref, sem):
    i, j = pl.program_id(0), pl.program_id(1)
    tm = h_ref.shape[0]
    n_valid = nv_ref[0]

    def start_gather(tile, slot):
        base = tile * tm

        def issue(r, carry):
            _row_copy(h_hbm, tok_ref[base + r], x_ref.at[slot], r, sem.at[slot]).start()
            return carry

        lax.fori_loop(0, tm, issue, 0, unroll=DMA_ISSUE_UNROLL)

    @pl.when((i == 0) & (j == 0))
    def _():
        start_gather(0, 0)

    @pl.when(i < n_valid)
    def _():
        @pl.when(j == 0)
        def _():
            slot = i % 2
            pltpu.make_async_copy(h_hbm.at[pl.ds(0, tm), :], x_ref.at[slot], sem.at[slot]).wait()
            h_ref[...] = x_ref[slot].astype(BF16)

            @pl.when(i + 1 < n_valid)
            def _():
                start_gather(i + 1, 1 - slot)

        h = h_ref[...]
        a = jnp.dot(h, wg_ref[...], preferred_element_type=F32)
        b = jnp.dot(h, wu_ref[...], preferred_element_type=F32)
        o_ref[...] = _silu_mul(a, b).astype(o_ref.dtype)

    @pl.when(i >= nv_ref[0])
    def _():
        o_ref[...] = jnp.zeros_like(o_ref)


def moe_up(h, tok, tile_expert, n_valid, wg, wu, tn=1024):
    d = h.shape[1]
    rows = tok.shape[0]
    f = wg.shape[2]
    tm, tn = MOE_TILE, min(tn, f)
    return pl.pallas_call(
        _moe_up_body,
        grid_spec=pltpu.PrefetchScalarGridSpec(
            num_scalar_prefetch=3,
            grid=(rows // tm, f // tn),
            in_specs=[pl.BlockSpec(memory_space=pl.ANY),
                      pl.BlockSpec((None, d, tn), lambda i, j, te, nv, tok: (te[i], 0, j)),
                      pl.BlockSpec((None, d, tn), lambda i, j, te, nv, tok: (te[i], 0, j))],
            out_specs=pl.BlockSpec((tm, tn), lambda i, j, te, nv, tok: (i, j)),
            scratch_shapes=[pltpu.VMEM((2, tm, d), F32),
                            pltpu.VMEM((tm, d), BF16),
                            pltpu.SemaphoreType.DMA((2,))]),
        out_shape=jax.ShapeDtypeStruct((rows, f), BF16),
        compiler_params=_params(("arbitrary", "arbitrary"), 48),
        name="moe_up",
    )(tile_expert, n_valid, tok, h, wg, wu)


def _moe_down_body(te_ref, nv_ref, a_ref, w_ref, o_ref):
    i = pl.program_id(0)

    @pl.when(i < nv_ref[0])
    def _():
        o_ref[...] = jnp.dot(a_ref[...], w_ref[...], preferred_element_type=F32)

    @pl.when(i >= nv_ref[0])
    def _():
        o_ref[...] = jnp.zeros_like(o_ref)


def moe_down(act, tile_expert, n_valid, wd, tn=512):
    rows, f = act.shape
    d = wd.shape[2]
    tm, tn = MOE_TILE, min(tn, d)
    return pl.pallas_call(
        _moe_down_body,
        grid_spec=pltpu.PrefetchScalarGridSpec(
            num_scalar_prefetch=2,
            grid=(rows // tm, d // tn),
            in_specs=[pl.BlockSpec((tm, f), lambda i, j, te, nv: (i, 0)),
                      pl.BlockSpec((None, f, tn), lambda i, j, te, nv: (te[i], 0, j))],
            out_specs=pl.BlockSpec((tm, tn), lambda i, j, te, nv: (i, j))),
        out_shape=jax.ShapeDtypeStruct((rows, d), F32),
        compiler_params=_params(("parallel", "arbitrary"), 56),
        name="moe_down",
    )(tile_expert, n_valid, act, wd)


def _combine_body(d1_ref, d2_ref, x_ref, meta_ref, g_ref, y_hbm, o_ref, buf_ref, sem):
    tc = x_ref.shape[0]
    base = pl.program_id(0) * tc

    def issue(r, carry):
        _row_copy(y_hbm, d1_ref[base + r], buf_ref.at[0], r, sem).start()
        _row_copy(y_hbm, d2_ref[base + r], buf_ref.at[1], r, sem).start()
        return carry

    lax.fori_loop(0, tc, issue, 0, unroll=DMA_ISSUE_UNROLL)
    for k in range(TOP_K):
        pltpu.make_async_copy(y_hbm.at[pl.ds(0, tc), :], buf_ref.at[k], sem).wait()
    meta = meta_ref[...]
    w1 = meta[:, META_W1:META_W1 + 1]
    w2 = meta[:, META_W2:META_W2 + 1]
    y = x_ref[...] + (w1 * buf_ref[0] + w2 * buf_ref[1])
    o_ref[...] = _rms(y, g_ref[...])


def moe_combine_norm(x, meta, d1, d2, y, g, tc=256):
    t, d = x.shape
    tc = min(tc, t)
    return pl.pallas_call(
        _combine_body,
        grid_spec=pltpu.PrefetchScalarGridSpec(
            num_scalar_prefetch=2,
            grid=(t // tc,),
            in_specs=[pl.BlockSpec((tc, d), lambda i, a, b: (i, 0)),
                      pl.BlockSpec((tc, LANES), lambda i, a, b: (i, 0)),
                      pl.BlockSpec((1, d), lambda i, a, b: (0, 0)),
                      pl.BlockSpec(memory_space=pl.ANY)],
            out_specs=pl.BlockSpec((tc, d), lambda i, a, b: (i, 0)),
            scratch_shapes=[pltpu.VMEM((2, tc, d), F32),
                            pltpu.SemaphoreType.DMA(())]),
        out_shape=jax.ShapeDtypeStruct((t, d), F32),
        compiler_params=_params(("arbitrary",), 32),
        name="moe_combine_norm",
    )(d1, d2, x, meta, g, y)


def moe_layer_and_final_norm(x, g, router_w, w_gate, w_up, w_down, final_g):
    t, d = x.shape
    n_experts = router_w.shape[1]
    tm = MOE_TILE
    rw = jnp.zeros((d, LANES), F32).at[:, :n_experts].set(router_w)
    h, meta, cnt = moe_router(x, g, rw, n_experts)

    rows = t * TOP_K + n_experts * tm
    n_tiles = rows // tm
    counts = cnt[0, :n_experts].astype(I32)
    tiles_per = (counts + tm - 1) // tm
    tile_end = jnp.cumsum(tiles_per)
    offs = (tile_end - tiles_per) * tm
    n_valid = tile_end[-1:]
    tile_expert = jnp.minimum(
        jnp.searchsorted(tile_end, jnp.arange(n_tiles, dtype=I32), side="right"),
        n_experts - 1).astype(I32)
    i1 = meta[:, META_I1].astype(I32)
    i2 = meta[:, META_I2].astype(I32)
    d1 = offs[i1] + meta[:, META_P1].astype(I32)
    d2 = offs[i2] + meta[:, META_P2].astype(I32)
    tid = jnp.arange(t, dtype=I32)
    tok = jnp.zeros((rows,), I32).at[jnp.concatenate([d1, d2])].set(jnp.concatenate([tid, tid]))

    act = moe_up(h, tok, tile_expert, n_valid, w_gate, w_up)
    y = moe_down(act, tile_expert, n_valid, w_down)
    return moe_combine_norm(x, meta, d1, d2, y, final_g)


def kernel(x, rel_bias, attn_norm_g, w_in, mix_norm_dil_g, mix_norm_sb_g, w_o, ffn_norm_g,
           dense_w_gate, dense_w_up, dense_w_down, router_w, moe_w_gate, moe_w_up,
           moe_w_down, final_norm_g):
    b, s, d = x.shape
    depth = w_in.shape[0]
    assert depth == 2, "layer 0 dense SwiGLU, layer 1 routed experts followed by the final norm"
    d_dil = mix_norm_dil_g.shape[1]
    d_sb = mix_norm_sb_g.shape[1]
    h_dil, h_sb = d_dil // HEAD_DIM, d_sb // HEAD_DIM
    tables = _dilated_bias_tables(rel_bias)
    row = lambda v: v.reshape(1, -1)

    xt = x.reshape(b * s, d)
    for layer in range(depth):
        w_in_l = w_in[layer].astype(BF16)
        g_attn = row(attn_norm_g[layer])
        p_dil, p_sb = norm_matmul_split(xt, g_attn, w_in_l, 3 * d_dil, F32, BF16)
        o_dil = dilated_attention(p_dil.reshape(b, s, 3 * d_dil), tables, h_dil)
        o_sb = stick_breaking_attention(p_sb.reshape(b, s, 3 * d_sb), h_sb)
        xt = mix_out(o_dil.reshape(b * s, d_dil),
                     tuple(o.reshape(b * s // 2, d_sb) for o in o_sb),
                     row(mix_norm_dil_g[layer]), row(mix_norm_sb_g[layer]),
                     w_o[layer].astype(BF16), xt, s)
        g_ffn = row(ffn_norm_g[layer])
        j = layer // 2
        if layer % 2 == 0:
            act = ffn_up(xt, g_ffn, dense_w_gate[j].astype(BF16), dense_w_up[j].astype(BF16))
            xt = matmul_residual(act, dense_w_down[j].astype(BF16), xt)
        else:
            xt = moe_layer_and_final_norm(
                xt, g_ffn, router_w[j], moe_w_gate[j].astype(BF16), moe_w_up[j].astype(BF16),
                moe_w_down[j].astype(BF16), row(final_norm_g))
    return xt.reshape(b, s, d)
```

```python
import functools

import numpy as np
import jax
import jax.numpy as jnp
from jax import lax
from jax.experimental import pallas as pl
from jax.experimental.pallas import tpu as pltpu

F32 = jnp.float32
BF16 = jnp.bfloat16
I32 = jnp.int32

EPS = 1e-6
HEAD_DIM = 128
DILATED_BRANCHES = ((128, 1), (512, 4), (2048, 16))
N_BUCKETS = 32
MAX_DISTANCE = 2048
TOP_K = 2
LANES = 128
WIN = 128
NEG_INF = float("-inf")
DIL_UNROLL = 8
MIB = 1024 * 1024


def _params(semantics, vmem_mib):
    return pltpu.CompilerParams(dimension_semantics=semantics,
                                vmem_limit_bytes=vmem_mib * MIB)


def _rms(x, g):
    return x * lax.rsqrt(jnp.mean(x * x, axis=-1, keepdims=True) + EPS) * g


def _norm_matmul_body(n_first, x_ref, g_ref, w_ref, oa_ref, ob_ref, h_ref):
    j = pl.program_id(1)

    @pl.when(j == 0)
    def _():
        h_ref[...] = _rms(x_ref[...], g_ref[...]).astype(BF16)

    y = jnp.dot(h_ref[...], w_ref[...], preferred_element_type=F32)

    @pl.when(j < n_first)
    def _():
        oa_ref[...] = y.astype(oa_ref.dtype)

    @pl.when(j >= n_first)
    def _():
        ob_ref[...] = y.astype(ob_ref.dtype)


def norm_matmul_split(x, g, w, n_a, dtype_a, dtype_b, tm=1024, tn=1024):
    t, d = x.shape
    n = w.shape[1]
    tm, tn = min(tm, t), min(tn, n_a, n - n_a)
    assert n_a % tn == 0 and (n - n_a) % tn == 0
    ja = n_a // tn
    return pl.pallas_call(
        functools.partial(_norm_matmul_body, ja),
        grid=(t // tm, n // tn),
        in_specs=[pl.BlockSpec((tm, d), lambda i, j: (i, 0)),
                  pl.BlockSpec((1, d), lambda i, j: (0, 0)),
                  pl.BlockSpec((d, tn), lambda i, j: (0, j))],
        out_specs=[pl.BlockSpec((tm, tn), lambda i, j: (i, jnp.minimum(j, ja - 1))),
                   pl.BlockSpec((tm, tn), lambda i, j: (i, jnp.maximum(j - ja, 0)))],
        out_shape=[jax.ShapeDtypeStruct((t, n_a), dtype_a),
                   jax.ShapeDtypeStruct((t, n - n_a), dtype_b)],
        scratch_shapes=[pltpu.VMEM((tm, d), BF16)],
        compiler_params=_params(("parallel", "arbitrary"), 56),
        name="norm_matmul",
    )(x, g, w)


def _silu_mul(a, b):
    return a * (1.0 / (1.0 + jnp.exp(-a))) * b


def _ffn_up_body(x_ref, g_ref, wg_ref, wu_ref, o_ref, h_ref):
    @pl.when(pl.program_id(1) == 0)
    def _():
        h_ref[...] = _rms(x_ref[...], g_ref[...]).astype(BF16)

    h = h_ref[...]
    a = jnp.dot(h, wg_ref[...].astype(BF16), preferred_element_type=F32)
    b = jnp.dot(h, wu_ref[...].astype(BF16), preferred_element_type=F32)
    o_ref[...] = _silu_mul(a, b).astype(o_ref.dtype)


def ffn_up(x, g, wg, wu, tm=1024, tn=512):
    t, d = x.shape
    f = wg.shape[1]
    tm, tn = min(tm, t), min(tn, f)
    return pl.pallas_call(
        _ffn_up_body,
        grid=(t // tm, pl.cdiv(f, tn)),
        in_specs=[pl.BlockSpec((tm, d), lambda i, j: (i, 0)),
                  pl.BlockSpec((1, d), lambda i, j: (0, 0)),
                  pl.BlockSpec((d, tn), lambda i, j: (0, j)),
                  pl.BlockSpec((d, tn), lambda i, j: (0, j))],
        out_specs=pl.BlockSpec((tm, tn), lambda i, j: (i, j)),
        out_shape=jax.ShapeDtypeStruct((t, f), BF16),
        scratch_shapes=[pltpu.VMEM((tm, d), BF16)],
        compiler_params=_params(("parallel", "arbitrary"), 56),
        name="ffn_up",
    )(x, g, wg, wu)


def _matmul_res_body(a_ref, w_ref, r_ref, o_ref):
    o_ref[...] = r_ref[...] + jnp.dot(a_ref[...], w_ref[...].astype(BF16), preferred_element_type=F32)


def matmul_residual(a, w, res, tm=512, tn=512):
    t, f = a.shape
    n = w.shape[1]
    tm, tn = min(tm, t), min(tn, n)
    return pl.pallas_call(
        _matmul_res_body,
        grid=(n // tn, t // tm),
        in_specs=[pl.BlockSpec((tm, f), lambda j, i: (i, 0)),
                  pl.BlockSpec((f, tn), lambda j, i: (0, j)),
                  pl.BlockSpec((tm, tn), lambda j, i: (i, j))],
        out_specs=pl.BlockSpec((tm, tn), lambda j, i: (i, j)),
        out_shape=jax.ShapeDtypeStruct((t, n), F32),
        compiler_params=_params(("parallel", "parallel"), 52),
        name="matmul_residual",
    )(a, w, res)


def _mix_out_body(tiles_per_seq, od_ref, sa_ref, sb_ref, gd_ref, gs_ref, w_ref, x_ref, o_ref, h_ref):
    dd = od_ref.shape[1]
    half = tiles_per_seq // 2
    si = pl.program_id(0) % tiles_per_seq

    @pl.when(pl.program_id(1) == 0)
    def _():
        h_ref[:, :dd] = _rms(od_ref[...].astype(F32), gd_ref[...]).astype(BF16)

        @pl.when(si < half)
        def _():
            h_ref[:, dd:] = _rms(sa_ref[...].astype(F32), gs_ref[...]).astype(BF16)

        @pl.when(si >= half)
        def _():
            h_ref[:, dd:] = _rms(sb_ref[...].astype(F32), gs_ref[...]).astype(BF16)

    o_ref[...] = x_ref[...] + jnp.dot(h_ref[...], w_ref[...], preferred_element_type=F32)


def mix_out(o_dil, o_sb_halves, g_dil, g_sb, w_o, x, seq, tm=1024, tn=1024):
    t, dd = o_dil.shape
    sa, sb = o_sb_halves
    ds = sa.shape[1]
    n = w_o.shape[1]
    tm, tn = min(tm, seq // 2), min(tn, n)
    tps = seq // tm
    half = tps // 2
    return pl.pallas_call(
        functools.partial(_mix_out_body, tps),
        grid=(t // tm, n // tn),
        in_specs=[pl.BlockSpec((tm, dd), lambda i, j: (i, 0)),
                  pl.BlockSpec((tm, ds), lambda i, j: ((i // tps) * half + jnp.minimum(i % tps, half - 1), 0)),
                  pl.BlockSpec((tm, ds), lambda i, j: ((i // tps) * half + jnp.maximum(i % tps - half, 0), 0)),
                  pl.BlockSpec((1, dd), lambda i, j: (0, 0)),
                  pl.BlockSpec((1, ds), lambda i, j: (0, 0)),
                  pl.BlockSpec((dd + ds, tn), lambda i, j: (0, j)),
                  pl.BlockSpec((tm, tn), lambda i, j: (i, j))],
        out_specs=pl.BlockSpec((tm, tn), lambda i, j: (i, j)),
        out_shape=jax.ShapeDtypeStruct((t, n), F32),
        scratch_shapes=[pltpu.VMEM((tm, dd + ds), BF16)],
        compiler_params=_params(("parallel", "arbitrary"), 56),
        name="mix_out",
    )(o_dil, sa, sb, g_dil, g_sb, w_o, x)


def _t5_bucket(dist):
    n = np.asarray(dist, dtype=np.int64)
    max_exact = N_BUCKETS // 2
    large = max_exact + (np.log(np.maximum(n, 1) / max_exact)
                         / np.log(MAX_DISTANCE / max_exact)
                         * (N_BUCKETS - max_exact)).astype(np.int64)
    large = np.minimum(large, N_BUCKETS - 1)
    return np.where(n < max_exact, n, large).astype(np.int32)


def _dilated_bias_tables(rel_bias):
    n_heads = rel_bias.shape[1]
    period = 3 * WIN
    pad = jnp.full((n_heads, period - WIN - 1), NEG_INF, F32)
    tables = []
    for (w, d) in DILATED_BRANCHES:
        assert w // d == WIN
        bias_m = rel_bias[_t5_bucket(d * np.arange(WIN, -1, -1))].astype(F32).T
        u = jnp.concatenate([bias_m, pad], axis=1)
        shifted = jnp.tile(u, (1, WIN))[:, :WIN * (period - 1)].reshape(n_heads, WIN, period - 1)
        tables.append(shifted[:, :, :2 * WIN])
    return jnp.stack(tables, axis=1)


def _dilated_body(q_ref, k_ref, v_ref, tb_ref, o_ref, ob_ref, lb_ref):
    s = q_ref.shape[0]
    scale = HEAD_DIM ** -0.5

    def load(ref, pos, d):
        return ref[pl.ds(pos, WIN, stride=d), :]

    def attend(bi, d, base, q, kk, vv, tb):
        sc = lax.dot_general(q, kk, (((1,), (1,)), ((), ())), preferred_element_type=F32) + tb
        m = jnp.max(sc, axis=-1, keepdims=True)
        p = jnp.exp(sc - m)
        l = jnp.sum(p, axis=-1, keepdims=True)
        out = jnp.dot(p.astype(BF16), vv, preferred_element_type=F32) * (1.0 / l)
        lse = m + jnp.log(l)
        ob_ref[bi, pl.ds(base, WIN, stride=d), :] = out
        lb_ref[bi, pl.ds(base, WIN, stride=d), :] = jnp.broadcast_to(lse, (WIN, LANES))

    def run(bi, d, start, nb, k_prev, v_prev):
        for b in range(nb):
            base = start + d * WIN * b
            q = (load(q_ref, base, d) * scale).astype(BF16)
            k_cur = load(k_ref, base, d).astype(BF16)
            v_cur = load(v_ref, base, d).astype(BF16)
            if k_prev is None:
                attend(bi, d, base, q, k_cur, v_cur, tb_ref[bi, :, WIN:])
            else:
                attend(bi, d, base, q, jnp.concatenate([k_prev, k_cur], axis=0),
                       jnp.concatenate([v_prev, v_cur], axis=0), tb_ref[bi])
            k_prev, v_prev = k_cur, v_cur

    for bi, (_, d) in enumerate(DILATED_BRANCHES):
        nblk = s // (d * WIN)
        if nblk <= DIL_UNROLL:
            per = DIL_UNROLL // nblk

            def trip(g, carry, bi=bi, d=d, nblk=nblk, per=per):
                for u in range(per):
                    run(bi, d, g * per + u, nblk, None, None)
                return carry

            lax.fori_loop(0, d // per, trip, 0)
        else:
            chunks = nblk // DIL_UNROLL
            span = d * WIN * DIL_UNROLL

            def head(r, carry, bi=bi, d=d):
                run(bi, d, r, DIL_UNROLL, None, None)
                return carry

            def tail(it, carry, bi=bi, d=d, chunks=chunks, span=span):
                start = it // (chunks - 1) + span * (it % (chunks - 1) + 1)
                run(bi, d, start, DIL_UNROLL, load(k_ref, start - d * WIN, d).astype(BF16),
                    load(v_ref, start - d * WIN, d).astype(BF16))
                return carry

            lax.fori_loop(0, d, head, 0)
            lax.fori_loop(0, d * (chunks - 1), tail, 0)

    ch = 256
    def merge(c, carry):
        rows = pl.ds(pl.multiple_of(c * ch, ch), ch)
        ls = [lb_ref[b, rows, :] for b in range(len(DILATED_BRANCHES))]
        mx = functools.reduce(jnp.maximum, ls)
        es = [jnp.exp(x - mx) for x in ls]
        inv = 1.0 / functools.reduce(jnp.add, es)
        acc = es[0] * inv * ob_ref[0, rows, :]
        for b in range(1, len(DILATED_BRANCHES)):
            acc = acc + es[b] * inv * ob_ref[b, rows, :]
        o_ref[rows, :] = acc.astype(o_ref.dtype)
        return carry

    lax.fori_loop(0, s // ch, merge, 0)


def dilated_attention(proj, tables, n_heads):
    b, s, _ = proj.shape
    h = n_heads
    nbr = len(DILATED_BRANCHES)
    return pl.pallas_call(
        _dilated_body,
        grid=(b, h),
        in_specs=[pl.BlockSpec((None, s, HEAD_DIM), lambda bi, hi: (bi, 0, hi)),
                  pl.BlockSpec((None, s, HEAD_DIM), lambda bi, hi: (bi, 0, h + hi)),
                  pl.BlockSpec((None, s, HEAD_DIM), lambda bi, hi: (bi, 0, 2 * h + hi)),
                  pl.BlockSpec((None, nbr, WIN, 2 * WIN), lambda bi, hi: (hi, 0, 0, 0))],
        out_specs=pl.BlockSpec((None, s, HEAD_DIM), lambda bi, hi: (bi, 0, hi)),
        out_shape=jax.ShapeDtypeStruct((b, s, h * HEAD_DIM), BF16),
        scratch_shapes=[pltpu.VMEM((nbr, s, HEAD_DIM), F32),
                        pltpu.VMEM((nbr, s, LANES), F32)],
        compiler_params=_params(("parallel", "parallel"), 48),
        name="dilated_attention",
    )(proj, proj, proj, tables)


SB_BLOCK = 256
LOG2E = 1.4426950408889634


def _softplus2(z2):
    return jnp.maximum(z2, 0.0) + jnp.log2(1.0 + jnp.exp2(-jnp.abs(z2)))


def _sb_body(nq, qa_ref, qb_ref, k_ref, v_ref, oa_ref, ob_ref, q_ref, acc_ref, c_ref):
    blk = SB_BLOCK
    p = pl.program_id(2)
    row = lax.broadcasted_iota(I32, (blk, blk), 0)
    col = lax.broadcasted_iota(I32, (blk, blk), 1)
    tri = (row >= col).astype(BF16)
    causal = col < row
    for slot, src in enumerate((qa_ref, qb_ref)):
        q_ref[slot] = (src[...].astype(F32) * (HEAD_DIM ** -0.5 * LOG2E)).astype(BF16)

    work = [(0, p, True), (1, nq - 1 - p, True)]
    for t in range(nq - 1):
        first = t < p
        work.append((jnp.where(first, 0, 1), jnp.where(first, p - 1 - t, nq - 2 - t), False))
    zs, vs, sums, cs = {}, {}, {}, {}

    def stage_scores(i):
        slot, kb, _ = work[i]
        rows = pl.ds(pl.multiple_of(kb * blk, blk), blk)
        zs[i] = lax.dot_general(q_ref[slot], k_ref[rows, :], (((1,), (1,)), ((), ())),
                                preferred_element_type=F32)
        vs[i] = v_ref[rows, :]

    def stage_sums(i):
        slot, _, diag = work[i]
        sp = _softplus2(zs[i])
        if diag:
            sp = jnp.where(causal, sp, 0.0)
        total = jnp.broadcast_to(jnp.sum(sp, axis=-1, keepdims=True), (blk, LANES))
        if diag:
            c_ref[slot] = total
        else:
            cs[i] = c_ref[slot]
            c_ref[slot] = cs[i] + total
        sums[i] = jnp.dot(sp.astype(BF16), tri, preferred_element_type=F32)

    def stage_out(i):
        slot, _, diag = work[i]
        e = zs.pop(i) - sums.pop(i)
        if diag:
            a = jnp.where(causal, jnp.exp2(e), 0.0)
        else:
            c = cs.pop(i)
            a = jnp.exp2(e - jnp.concatenate([c, c], axis=1))
        pv = jnp.dot(a.astype(BF16), vs.pop(i), preferred_element_type=F32)
        if diag:
            acc_ref[slot] = pv
        else:
            acc_ref[slot] += pv

    n = len(work)
    for step in range(n + 2):
        if step < n:
            stage_scores(step)
        if 0 <= step - 1 < n:
            stage_sums(step - 1)
        if 0 <= step - 2 < n:
            stage_out(step - 2)
    oa_ref[...] = acc_ref[0].astype(oa_ref.dtype)
    ob_ref[...] = acc_ref[1].astype(ob_ref.dtype)


def stick_breaking_attention(proj, n_heads):
    b, s, _ = proj.shape
    h = n_heads
    blk = SB_BLOCK
    nq = s // blk
    assert LANES * 2 == blk and nq % 2 == 0
    q_spec = lambda f: pl.BlockSpec((None, blk, HEAD_DIM), lambda bi, hi, p: (bi, f(p), hi))
    oa, ob = pl.pallas_call(
        functools.partial(_sb_body, nq),
        grid=(b, h, nq // 2),
        in_specs=[q_spec(lambda p: p), q_spec(lambda p: nq - 1 - p),
                  pl.BlockSpec((None, s, HEAD_DIM), lambda bi, hi, p: (bi, 0, h + hi)),
                  pl.BlockSpec((None, s, HEAD_DIM), lambda bi, hi, p: (bi, 0, 2 * h + hi))],
        out_specs=[q_spec(lambda p: p), q_spec(lambda p: nq // 2 - 1 - p)],
        out_shape=[jax.ShapeDtypeStruct((b, s // 2, h * HEAD_DIM), BF16)] * 2,
        scratch_shapes=[pltpu.VMEM((2, blk, HEAD_DIM), BF16),
                        pltpu.VMEM((2, blk, HEAD_DIM), F32),
                        pltpu.VMEM((2, blk, LANES), F32)],
        compiler_params=_params(("parallel", "parallel", "parallel"), 32),
        name="stick_breaking_attention",
    )(proj, proj, proj, proj)
    return oa, ob


MOE_TILE = 512
DMA_ISSUE_UNROLL = 8
META_I1, META_I2, META_W1, META_W2, META_P1, META_P2 = range(6)


def _router_body(n_experts, x_ref, g_ref, rw_ref, h_ref, meta_ref, cnt_ref, carry_ref):
    tm = x_ref.shape[0]

    @pl.when(pl.program_id(0) == 0)
    def _():
        carry_ref[...] = jnp.zeros_like(carry_ref)

    h = _rms(x_ref[...], g_ref[...])
    h_ref[...] = h
    logits = jnp.dot(h, rw_ref[...], precision=lax.Precision.HIGHEST,
                     preferred_element_type=F32)
    lane = lax.broadcasted_iota(I32, logits.shape, 1)
    l1 = jnp.where(lane < n_experts, logits, NEG_INF)
    m1 = jnp.max(l1, axis=-1, keepdims=True)
    i1 = jnp.min(jnp.where(l1 == m1, lane, LANES), axis=-1, keepdims=True)
    l2 = jnp.where(lane == i1, NEG_INF, l1)
    m2 = jnp.max(l2, axis=-1, keepdims=True)
    i2 = jnp.min(jnp.where(l2 == m2, lane, LANES), axis=-1, keepdims=True)
    e2 = jnp.exp(m2 - m1)
    inv = 1.0 / (1.0 + e2)
    w1, w2 = inv, e2 * inv
    sel1, sel2 = lane == i1, lane == i2
    onehot = jnp.where(sel1 | sel2, 1.0, 0.0)
    r = lax.broadcasted_iota(I32, (tm, tm), 0)
    c = lax.broadcasted_iota(I32, (tm, tm), 1)
    before = (c < r).astype(BF16)
    pos = jnp.dot(before, onehot.astype(BF16), preferred_element_type=F32) + carry_ref[...]
    p1 = jnp.sum(jnp.where(sel1, pos, 0.0), axis=-1, keepdims=True)
    p2 = jnp.sum(jnp.where(sel2, pos, 0.0), axis=-1, keepdims=True)
    meta = jnp.zeros_like(logits)
    for idx, val in ((META_I1, i1.astype(F32)), (META_I2, i2.astype(F32)), (META_W1, w1),
                     (META_W2, w2), (META_P1, p1), (META_P2, p2)):
        meta = jnp.where(lane == idx, val, meta)
    meta_ref[...] = meta
    carry_ref[...] += jnp.sum(onehot, axis=0, keepdims=True)
    cnt_ref[...] = carry_ref[...]


def moe_router(x, g, rw, n_experts, tm=512):
    t, d = x.shape
    tm = min(tm, t)
    return pl.pallas_call(
        functools.partial(_router_body, n_experts),
        grid=(t // tm,),
        in_specs=[pl.BlockSpec((tm, d), lambda i: (i, 0)),
                  pl.BlockSpec((1, d), lambda i: (0, 0)),
                  pl.BlockSpec((d, LANES), lambda i: (0, 0))],
        out_specs=[pl.BlockSpec((tm, d), lambda i: (i, 0)),
                   pl.BlockSpec((tm, LANES), lambda i: (i, 0)),
                   pl.BlockSpec((1, LANES), lambda i: (0, 0))],
        out_shape=[jax.ShapeDtypeStruct((t, d), F32),
                   jax.ShapeDtypeStruct((t, LANES), F32),
                   jax.ShapeDtypeStruct((1, LANES), F32)],
        scratch_shapes=[pltpu.VMEM((1, LANES), F32)],
        compiler_params=_params(("arbitrary",), 32),
        name="moe_router",
    )(x, g, rw)


def _row_copy(src_hbm, src_row, dst_ref, dst_row, sem):
    return pltpu.make_async_copy(src_hbm.at[pl.ds(src_row, 1), :],
                                 dst_ref.at[pl.ds(dst_row, 1), :], sem)


def _moe_up_body(te_ref, nv_ref, tok_ref, h_hbm, wg_ref, wu_ref, o_ref, x_ref, h_ref, sem):
    i, j = pl.program_id(0), pl.program_id(1)
    tm = h_ref.shape[0]
    n_valid = nv_ref[0]

    def start_gather(tile, slot):
        base = tile * tm

        def issue(r, carry):
            _row_copy(h_hbm, tok_ref[base + r], x_ref.at[slot], r, sem.at[slot]).start()
            return carry

        lax.fori_loop(0, tm, issue, 0, unroll=DMA_ISSUE_UNROLL)

    @pl.when((i == 0) & (j == 0))
    def _():
        start_gather(0, 0)

    @pl.when(i < n_valid)
    def _():
        @pl.when(j == 0)
        def _():
            slot = i % 2
            pltpu.make_async_copy(h_hbm.at[pl.ds(0, tm), :], x_ref.at[slot], sem.at[slot]).wait()
            h_ref[...] = x_ref[slot].astype(BF16)

            @pl.when(i + 1 < n_valid)
            def _():
                start_gather(i + 1, 1 - slot)

        h = h_ref[...]
        a = jnp.dot(h, wg_ref[...].astype(BF16), preferred_element_type=F32)
        b = jnp.dot(h, wu_ref[...].astype(BF16), preferred_element_type=F32)
        o_ref[...] = _silu_mul(a, b).astype(o_ref.dtype)

    @pl.when(i >= nv_ref[0])
    def _():
        o_ref[...] = jnp.zeros_like(o_ref)


def moe_up(h, tok, tile_expert, n_valid, wg, wu, tn=512):
    d = h.shape[1]
    rows = tok.shape[0]
    f = wg.shape[2]
    tm, tn = MOE_TILE, min(tn, f)
    return pl.pallas_call(
        _moe_up_body,
        grid_spec=pltpu.PrefetchScalarGridSpec(
            num_scalar_prefetch=3,
            grid=(rows // tm, f // tn),
            in_specs=[pl.BlockSpec(memory_space=pl.ANY),
                      pl.BlockSpec((None, d, tn), lambda i, j, te, nv, tok: (te[i], 0, j)),
                      pl.BlockSpec((None, d, tn), lambda i, j, te, nv, tok: (te[i], 0, j))],
            out_specs=pl.BlockSpec((tm, tn), lambda i, j, te, nv, tok: (i, j)),
            scratch_shapes=[pltpu.VMEM((2, tm, d), F32),
                            pltpu.VMEM((tm, d), BF16),
                            pltpu.SemaphoreType.DMA((2,))]),
        out_shape=jax.ShapeDtypeStruct((rows, f), BF16),
        compiler_params=_params(("arbitrary", "arbitrary"), 48),
        name="moe_up",
    )(tile_expert, n_valid, tok, h, wg, wu)


def _moe_down_body(te_ref, nv_ref, a_ref, w_ref, o_ref):
    i = pl.program_id(0)

    @pl.when(i < nv_ref[0])
    def _():
        o_ref[...] = jnp.dot(a_ref[...], w_ref[...].astype(BF16), preferred_element_type=F32)

    @pl.when(i >= nv_ref[0])
    def _():
        o_ref[...] = jnp.zeros_like(o_ref)


def moe_down(act, tile_expert, n_valid, wd, tn=256):
    rows, f = act.shape
    d = wd.shape[2]
    tm, tn = MOE_TILE, min(tn, d)
    return pl.pallas_call(
        _moe_down_body,
        grid_spec=pltpu.PrefetchScalarGridSpec(
            num_scalar_prefetch=2,
            grid=(rows // tm, d // tn),
            in_specs=[pl.BlockSpec((tm, f), lambda i, j, te, nv: (i, 0)),
                      pl.BlockSpec((None, f, tn), lambda i, j, te, nv: (te[i], 0, j))],
            out_specs=pl.BlockSpec((tm, tn), lambda i, j, te, nv: (i, j))),
        out_shape=jax.ShapeDtypeStruct((rows, d), F32),
        compiler_params=_params(("parallel", "arbitrary"), 56),
        name="moe_down",
    )(tile_expert, n_valid, act, wd)


def _combine_body(d1_ref, d2_ref, x_ref, meta_ref, g_ref, y_hbm, o_ref, buf_ref, sem):
    tc = x_ref.shape[0]
    base = pl.program_id(0) * tc

    def issue(r, carry):
        _row_copy(y_hbm, d1_ref[base + r], buf_ref.at[0], r, sem).start()
        _row_copy(y_hbm, d2_ref[base + r], buf_ref.at[1], r, sem).start()
        return carry

    lax.fori_loop(0, tc, issue, 0, unroll=DMA_ISSUE_UNROLL)
    for k in range(TOP_K):
        pltpu.make_async_copy(y_hbm.at[pl.ds(0, tc), :], buf_ref.at[k], sem).wait()
    meta = meta_ref[...]
    w1 = meta[:, META_W1:META_W1 + 1]
    w2 = meta[:, META_W2:META_W2 + 1]
    y = x_ref[...] + (w1 * buf_ref[0] + w2 * buf_ref[1])
    o_ref[...] = _rms(y, g_ref[...])


def moe_combine_norm(x, meta, d1, d2, y, g, tc=256):
    t, d = x.shape
    tc = min(tc, t)
    return pl.pallas_call(
        _combine_body,
        grid_spec=pltpu.PrefetchScalarGridSpec(
            num_scalar_prefetch=2,
            grid=(t // tc,),
            in_specs=[pl.BlockSpec((tc, d), lambda i, a, b: (i, 0)),
                      pl.BlockSpec((tc, LANES), lambda i, a, b: (i, 0)),
                      pl.BlockSpec((1, d), lambda i, a, b: (0, 0)),
                      pl.BlockSpec(memory_space=pl.ANY)],
            out_specs=pl.BlockSpec((tc, d), lambda i, a, b: (i, 0)),
            scratch_shapes=[pltpu.VMEM((2, tc, d), F32),
                            pltpu.SemaphoreType.DMA(())]),
        out_shape=jax.ShapeDtypeStruct((t, d), F32),
        compiler_params=_params(("arbitrary",), 32),
        name="moe_combine_norm",
    )(d1, d2, x, meta, g, y)


def moe_layer_and_final_norm(x, g, router_w, w_gate, w_up, w_down, final_g):
    t, d = x.shape
    n_experts = router_w.shape[1]
    tm = MOE_TILE
    rw = jnp.zeros((d, LANES), F32).at[:, :n_experts].set(router_w)
    h, meta, cnt = moe_router(x, g, rw, n_experts)

    rows = t * TOP_K + n_experts * tm
    n_tiles = rows // tm
    counts = cnt[0, :n_experts].astype(I32)
    tiles_per = (counts + tm - 1) // tm
    tile_end = jnp.cumsum(tiles_per)
    offs = (tile_end - tiles_per) * tm
    n_valid = tile_end[-1:]
    tile_expert = jnp.minimum(
        jnp.searchsorted(tile_end, jnp.arange(n_tiles, dtype=I32), side="right"),
        n_experts - 1).astype(I32)
    i1 = meta[:, META_I1].astype(I32)
    i2 = meta[:, META_I2].astype(I32)
    d1 = offs[i1] + meta[:, META_P1].astype(I32)
    d2 = offs[i2] + meta[:, META_P2].astype(I32)
    tid = jnp.arange(t, dtype=I32)
    tok = jnp.zeros((rows,), I32).at[jnp.concatenate([d1, d2])].set(jnp.concatenate([tid, tid]))

    act = moe_up(h, tok, tile_expert, n_valid, w_gate, w_up)
    y = moe_down(act, tile_expert, n_valid, w_down)
    return moe_combine_norm(x, meta, d1, d2, y, final_g)


def kernel(x, rel_bias, attn_norm_g, w_in, mix_norm_dil_g, mix_norm_sb_g, w_o, ffn_norm_g,
           dense_w_gate, dense_w_up, dense_w_down, router_w, moe_w_gate, moe_w_up,
           moe_w_down, final_norm_g):
    b, s, d = x.shape
    depth = w_in.shape[0]
    assert depth == 2, "layer 0 dense SwiGLU, layer 1 routed experts followed by the final norm"
    d_dil = mix_norm_dil_g.shape[1]
    d_sb = mix_norm_sb_g.shape[1]
    h_dil, h_sb = d_dil // HEAD_DIM, d_sb // HEAD_DIM
    tables = _dilated_bias_tables(rel_bias)
    row = lambda v: v.reshape(1, -1)

    xt = x.reshape(b * s, d)
    for layer in range(depth):
        w_in_l = w_in[layer].astype(BF16)
        g_attn = row(attn_norm_g[layer])
        p_dil, p_sb = norm_matmul_split(xt, g_attn, w_in_l, 3 * d_dil, F32, BF16)
        o_dil = dilated_attention(p_dil.reshape(b, s, 3 * d_dil), tables, h_dil)
        o_sb = stick_breaking_attention(p_sb.reshape(b, s, 3 * d_sb), h_sb)
        xt = mix_out(o_dil.reshape(b * s, d_dil),
                     tuple(o.reshape(b * s // 2, d_sb) for o in o_sb),
                     row(mix_norm_dil_g[layer]), row(mix_norm_sb_g[layer]),
                     w_o[layer].astype(BF16), xt, s)
        g_ffn = row(ffn_norm_g[layer])
        j = layer // 2
        if layer % 2 == 0:
            act = ffn_up(xt, g_ffn, dense_w_gate[j], dense_w_up[j])
            xt = matmul_residual(act, dense_w_down[j], xt)
        else:
            xt = moe_layer_and_final_norm(
                xt, g_ffn, router_w[j], moe_w_gate[j], moe_w_up[j], moe_w_down[j],
                row(final_norm_g))
    return xt.reshape(b, s, d)
```

```python
import functools

import numpy as np
import jax
import jax.numpy as jnp
from jax import lax
from jax.experimental import pallas as pl
from jax.experimental.pallas import tpu as pltpu

F32 = jnp.float32
BF16 = jnp.bfloat16
I32 = jnp.int32

EPS = 1e-6
HEAD_DIM = 128
DILATED_BRANCHES = ((128, 1), (512, 4), (2048, 16))
N_BUCKETS = 32
MAX_DISTANCE = 2048
TOP_K = 2
LANES = 128
WIN = 128
NEG_INF = float("-inf")
DIL_UNROLL = 8
MIB = 1024 * 1024


def _params(semantics, vmem_mib):
    return pltpu.CompilerParams(dimension_semantics=semantics,
                                vmem_limit_bytes=vmem_mib * MIB)


def _rms(x, g):
    return x * lax.rsqrt(jnp.mean(x * x, axis=-1, keepdims=True) + EPS) * g


def _norm_matmul_body(n_first, x_ref, g_ref, w_ref, oa_ref, ob_ref, h_ref):
    j = pl.program_id(1)

    @pl.when(j == 0)
    def _():
        h_ref[...] = _rms(x_ref[...], g_ref[...]).astype(BF16)

    y = jnp.dot(h_ref[...], w_ref[...], preferred_element_type=F32)

    @pl.when(j < n_first)
    def _():
        oa_ref[...] = y.astype(oa_ref.dtype)

    @pl.when(j >= n_first)
    def _():
        ob_ref[...] = y.astype(ob_ref.dtype)


def norm_matmul_split(x, g, w, n_a, dtype_a, dtype_b, tm=1024, tn=1024):
    t, d = x.shape
    n = w.shape[1]
    tm, tn = min(tm, t), min(tn, n_a, n - n_a)
    assert n_a % tn == 0 and (n - n_a) % tn == 0
    ja = n_a // tn
    return pl.pallas_call(
        functools.partial(_norm_matmul_body, ja),
        grid=(t // tm, n // tn),
        in_specs=[pl.BlockSpec((tm, d), lambda i, j: (i, 0)),
                  pl.BlockSpec((1, d), lambda i, j: (0, 0)),
                  pl.BlockSpec((d, tn), lambda i, j: (0, j))],
        out_specs=[pl.BlockSpec((tm, tn), lambda i, j: (i, jnp.minimum(j, ja - 1))),
                   pl.BlockSpec((tm, tn), lambda i, j: (i, jnp.maximum(j - ja, 0)))],
        out_shape=[jax.ShapeDtypeStruct((t, n_a), dtype_a),
                   jax.ShapeDtypeStruct((t, n - n_a), dtype_b)],
        scratch_shapes=[pltpu.VMEM((tm, d), BF16)],
        compiler_params=_params(("parallel", "arbitrary"), 56),
        name="norm_matmul",
    )(x, g, w)


def _silu_mul(a, b):
    return a * (1.0 / (1.0 + jnp.exp(-a))) * b


def _ffn_up_body(x_ref, g_ref, wg_ref, wu_ref, o_ref, h_ref):
    @pl.when(pl.program_id(1) == 0)
    def _():
        h_ref[...] = _rms(x_ref[...], g_ref[...]).astype(BF16)

    h = h_ref[...]
    a = jnp.dot(h, wg_ref[...], preferred_element_type=F32)
    b = jnp.dot(h, wu_ref[...], preferred_element_type=F32)
    o_ref[...] = _silu_mul(a, b).astype(o_ref.dtype)


def ffn_up(x, g, wg, wu, tm=1024, tn=512):
    t, d = x.shape
    f = wg.shape[1]
    tm, tn = min(tm, t), min(tn, f)
    return pl.pallas_call(
        _ffn_up_body,
        grid=(t // tm, pl.cdiv(f, tn)),
        in_specs=[pl.BlockSpec((tm, d), lambda i, j: (i, 0)),
                  pl.BlockSpec((1, d), lambda i, j: (0, 0)),
                  pl.BlockSpec((d, tn), lambda i, j: (0, j)),
                  pl.BlockSpec((d, tn), lambda i, j: (0, j))],
        out_specs=pl.BlockSpec((tm, tn), lambda i, j: (i, j)),
        out_shape=jax.ShapeDtypeStruct((t, f), BF16),
        scratch_shapes=[pltpu.VMEM((tm, d), BF16)],
        compiler_params=_params(("parallel", "arbitrary"), 56),
        name="ffn_up",
    )(x, g, wg, wu)


def _matmul_res_body(a_ref, w_ref, r_ref, o_ref):
    o_ref[...] = r_ref[...] + jnp.dot(a_ref[...], w_ref[...], preferred_element_type=F32)


def matmul_residual(a, w, res, tm=512, tn=1024):
    t, f = a.shape
    n = w.shape[1]
    tm, tn = min(tm, t), min(tn, n)
    return pl.pallas_call(
        _matmul_res_body,
        grid=(n // tn, t // tm),
        in_specs=[pl.BlockSpec((tm, f), lambda j, i: (i, 0)),
                  pl.BlockSpec((f, tn), lambda j, i: (0, j)),
                  pl.BlockSpec((tm, tn), lambda j, i: (i, j))],
        out_specs=pl.BlockSpec((tm, tn), lambda j, i: (i, j)),
        out_shape=jax.ShapeDtypeStruct((t, n), F32),
        compiler_params=_params(("parallel", "parallel"), 52),
        name="matmul_residual",
    )(a, w, res)


def _mix_out_body(tiles_per_seq, od_ref, sa_ref, sb_ref, gd_ref, gs_ref, w_ref, x_ref, o_ref, h_ref):
    dd = od_ref.shape[1]
    half = tiles_per_seq // 2
    si = pl.program_id(0) % tiles_per_seq

    @pl.when(pl.program_id(1) == 0)
    def _():
        h_ref[:, :dd] = _rms(od_ref[...].astype(F32), gd_ref[...]).astype(BF16)

        @pl.when(si < half)
        def _():
            h_ref[:, dd:] = _rms(sa_ref[...].astype(F32), gs_ref[...]).astype(BF16)

        @pl.when(si >= half)
        def _():
            h_ref[:, dd:] = _rms(sb_ref[...].astype(F32), gs_ref[...]).astype(BF16)

    o_ref[...] = x_ref[...] + jnp.dot(h_ref[...], w_ref[...], preferred_element_type=F32)


def mix_out(o_dil, o_sb_halves, g_dil, g_sb, w_o, x, seq, tm=1024, tn=1024):
    t, dd = o_dil.shape
    sa, sb = o_sb_halves
    ds = sa.shape[1]
    n = w_o.shape[1]
    tm, tn = min(tm, seq // 2), min(tn, n)
    tps = seq // tm
    half = tps // 2
    return pl.pallas_call(
        functools.partial(_mix_out_body, tps),
        grid=(t // tm, n // tn),
        in_specs=[pl.BlockSpec((tm, dd), lambda i, j: (i, 0)),
                  pl.BlockSpec((tm, ds), lambda i, j: ((i // tps) * half + jnp.minimum(i % tps, half - 1), 0)),
                  pl.BlockSpec((tm, ds), lambda i, j: ((i // tps) * half + jnp.maximum(i % tps - half, 0), 0)),
                  pl.BlockSpec((1, dd), lambda i, j: (0, 0)),
                  pl.BlockSpec((1, ds), lambda i, j: (0, 0)),
                  pl.BlockSpec((dd + ds, tn), lambda i, j: (0, j)),
                  pl.BlockSpec((tm, tn), lambda i, j: (i, j))],
        out_specs=pl.BlockSpec((tm, tn), lambda i, j: (i, j)),
        out_shape=jax.ShapeDtypeStruct((t, n), F32),
        scratch_shapes=[pltpu.VMEM((tm, dd + ds), BF16)],
        compiler_params=_params(("parallel", "arbitrary"), 56),
        name="mix_out",
    )(o_dil, sa, sb, g_dil, g_sb, w_o, x)


def _t5_bucket(dist):
    n = np.asarray(dist, dtype=np.int64)
    max_exact = N_BUCKETS // 2
    large = max_exact + (np.log(np.maximum(n, 1) / max_exact)
                         / np.log(MAX_DISTANCE / max_exact)
                         * (N_BUCKETS - max_exact)).astype(np.int64)
    large = np.minimum(large, N_BUCKETS - 1)
    return np.where(n < max_exact, n, large).astype(np.int32)


def _dilated_bias_tables(rel_bias):
    n_heads = rel_bias.shape[1]
    period = 3 * WIN
    pad = jnp.full((n_heads, period - WIN - 1), NEG_INF, F32)
    tables = []
    for (w, d) in DILATED_BRANCHES:
        assert w // d == WIN
        bias_m = rel_bias[_t5_bucket(d * np.arange(WIN, -1, -1))].astype(F32).T
        u = jnp.concatenate([bias_m, pad], axis=1)
        shifted = jnp.tile(u, (1, WIN))[:, :WIN * (period - 1)].reshape(n_heads, WIN, period - 1)
        tables.append(shifted[:, :, :2 * WIN])
    return jnp.stack(tables, axis=1)


def _dilated_body(q_ref, k_ref, v_ref, tb_ref, o_ref, ob_ref, lb_ref):
    s = q_ref.shape[0]
    scale = HEAD_DIM ** -0.5

    def load(ref, pos, d):
        return ref[pl.ds(pos, WIN, stride=d), :]

    def attend(bi, d, base, q, kk, vv, tb):
        sc = lax.dot_general(q, kk, (((1,), (1,)), ((), ())), preferred_element_type=F32) + tb
        m = jnp.max(sc, axis=-1, keepdims=True)
        p = jnp.exp(sc - m)
        l = jnp.sum(p, axis=-1, keepdims=True)
        out = jnp.dot(p.astype(BF16), vv, preferred_element_type=F32) * (1.0 / l)
        lse = m + jnp.log(l)
        ob_ref[bi, pl.ds(base, WIN, stride=d), :] = out
        lb_ref[bi, pl.ds(base, WIN, stride=d), :] = jnp.broadcast_to(lse, (WIN, LANES))

    def run(bi, d, start, nb, k_prev, v_prev):
        for b in range(nb):
            base = start + d * WIN * b
            q = (load(q_ref, base, d) * scale).astype(BF16)
            k_cur = load(k_ref, base, d).astype(BF16)
            v_cur = load(v_ref, base, d).astype(BF16)
            if k_prev is None:
                attend(bi, d, base, q, k_cur, v_cur, tb_ref[bi, :, WIN:])
            else:
                attend(bi, d, base, q, jnp.concatenate([k_prev, k_cur], axis=0),
                       jnp.concatenate([v_prev, v_cur], axis=0), tb_ref[bi])
            k_prev, v_prev = k_cur, v_cur

    for bi, (_, d) in enumerate(DILATED_BRANCHES):
        nblk = s // (d * WIN)
        if nblk <= DIL_UNROLL:
            per = DIL_UNROLL // nblk

            def trip(g, carry, bi=bi, d=d, nblk=nblk, per=per):
                for u in range(per):
                    run(bi, d, g * per + u, nblk, None, None)
                return carry

            lax.fori_loop(0, d // per, trip, 0)
        else:
            chunks = nblk // DIL_UNROLL
            span = d * WIN * DIL_UNROLL

            def head(r, carry, bi=bi, d=d):
                run(bi, d, r, DIL_UNROLL, None, None)
                return carry

            def tail(it, carry, bi=bi, d=d, chunks=chunks, span=span):
                start = it // (chunks - 1) + span * (it % (chunks - 1) + 1)
                run(bi, d, start, DIL_UNROLL, load(k_ref, start - d * WIN, d).astype(BF16),
                    load(v_ref, start - d * WIN, d).astype(BF16))
                return carry

            lax.fori_loop(0, d, head, 0)
            lax.fori_loop(0, d * (chunks - 1), tail, 0)

    ch = 256
    def merge(c, carry):
        rows = pl.ds(pl.multiple_of(c * ch, ch), ch)
        ls = [lb_ref[b, rows, :] for b in range(len(DILATED_BRANCHES))]
        mx = functools.reduce(jnp.maximum, ls)
        es = [jnp.exp(x - mx) for x in ls]
        inv = 1.0 / functools.reduce(jnp.add, es)
        acc = es[0] * inv * ob_ref[0, rows, :]
        for b in range(1, len(DILATED_BRANCHES)):
            acc = acc + es[b] * inv * ob_ref[b, rows, :]
        o_ref[rows, :] = acc.astype(o_ref.dtype)
        return carry

    lax.fori_loop(0, s // ch, merge, 0)


def dilated_attention(proj, tables, n_heads):
    b, s, _ = proj.shape
    h = n_heads
    nbr = len(DILATED_BRANCHES)
    return pl.pallas_call(
        _dilated_body,
        grid=(b, h),
        in_specs=[pl.BlockSpec((None, s, HEAD_DIM), lambda bi, hi: (bi, 0, hi)),
                  pl.BlockSpec((None, s, HEAD_DIM), lambda bi, hi: (bi, 0, h + hi)),
                  pl.BlockSpec((None, s, HEAD_DIM), lambda bi, hi: (bi, 0, 2 * h + hi)),
                  pl.BlockSpec((None, nbr, WIN, 2 * WIN), lambda bi, hi: (hi, 0, 0, 0))],
        out_specs=pl.BlockSpec((None, s, HEAD_DIM), lambda bi, hi: (bi, 0, hi)),
        out_shape=jax.ShapeDtypeStruct((b, s, h * HEAD_DIM), BF16),
        scratch_shapes=[pltpu.VMEM((nbr, s, HEAD_DIM), F32),
                        pltpu.VMEM((nbr, s, LANES), F32)],
        compiler_params=_params(("parallel", "parallel"), 48),
        name="dilated_attention",
    )(proj, proj, proj, tables)


SB_BLOCK = 256
LOG2E = 1.4426950408889634


def _softplus2(z2):
    return jnp.maximum(z2, 0.0) + jnp.log2(1.0 + jnp.exp2(-jnp.abs(z2)))


def _sb_body(nq, qa_ref, qb_ref, k_ref, v_ref, oa_ref, ob_ref, q_ref, acc_ref, c_ref):
    blk = SB_BLOCK
    p = pl.program_id(2)
    row = lax.broadcasted_iota(I32, (blk, blk), 0)
    col = lax.broadcasted_iota(I32, (blk, blk), 1)
    tri = (row >= col).astype(BF16)
    causal = col < row
    for slot, src in enumerate((qa_ref, qb_ref)):
        q_ref[slot] = (src[...].astype(F32) * (HEAD_DIM ** -0.5 * LOG2E)).astype(BF16)

    work = [(0, p, True), (1, nq - 1 - p, True)]
    for t in range(nq - 1):
        first = t < p
        work.append((jnp.where(first, 0, 1), jnp.where(first, p - 1 - t, nq - 2 - t), False))
    zs, vs, sums, cs = {}, {}, {}, {}

    def stage_scores(i):
        slot, kb, _ = work[i]
        rows = pl.ds(pl.multiple_of(kb * blk, blk), blk)
        zs[i] = lax.dot_general(q_ref[slot], k_ref[rows, :], (((1,), (1,)), ((), ())),
                                preferred_element_type=F32)
        vs[i] = v_ref[rows, :]

    def stage_sums(i):
        slot, _, diag = work[i]
        sp = _softplus2(zs[i])
        if diag:
            sp = jnp.where(causal, sp, 0.0)
        total = jnp.broadcast_to(jnp.sum(sp, axis=-1, keepdims=True), (blk, LANES))
        if diag:
            c_ref[slot] = total
        else:
            cs[i] = c_ref[slot]
            c_ref[slot] = cs[i] + total
        sums[i] = jnp.dot(sp.astype(BF16), tri, preferred_element_type=F32)

    def stage_out(i):
        slot, _, diag = work[i]
        e = zs.pop(i) - sums.pop(i)
        if diag:
            a = jnp.where(causal, jnp.exp2(e), 0.0)
        else:
            c = cs.pop(i)
            a = jnp.exp2(e - jnp.concatenate([c, c], axis=1))
        pv = jnp.dot(a.astype(BF16), vs.pop(i), preferred_element_type=F32)
        if diag:
            acc_ref[slot] = pv
        else:
            acc_ref[slot] += pv

    n = len(work)
    for step in range(n + 2):
        if step < n:
            stage_scores(step)
        if 0 <= step - 1 < n:
            stage_sums(step - 1)
        if 0 <= step - 2 < n:
            stage_out(step - 2)
    oa_ref[...] = acc_ref[0].astype(oa_ref.dtype)
    ob_ref[...] = acc_ref[1].astype(ob_ref.dtype)


def stick_breaking_attention(proj, n_heads):
    b, s, _ = proj.shape
    h = n_heads
    blk = SB_BLOCK
    nq = s // blk
    assert LANES * 2 == blk and nq % 2 == 0
    q_spec = lambda f: pl.BlockSpec((None, blk, HEAD_DIM), lambda bi, hi, p: (bi, f(p), hi))
    oa, ob = pl.pallas_call(
        functools.partial(_sb_body, nq),
        grid=(b, h, nq // 2),
        in_specs=[q_spec(lambda p: p), q_spec(lambda p: nq - 1 - p),
                  pl.BlockSpec((None, s, HEAD_DIM), lambda bi, hi, p: (bi, 0, h + hi)),
                  pl.BlockSpec((None, s, HEAD_DIM), lambda bi, hi, p: (bi, 0, 2 * h + hi))],
        out_specs=[q_spec(lambda p: p), q_spec(lambda p: nq // 2 - 1 - p)],
        out_shape=[jax.ShapeDtypeStruct((b, s // 2, h * HEAD_DIM), BF16)] * 2,
        scratch_shapes=[pltpu.VMEM((2, blk, HEAD_DIM), BF16),
                        pltpu.VMEM((2, blk, HEAD_DIM), F32),
                        pltpu.VMEM((2, blk, LANES), F32)],
        compiler_params=_params(("parallel", "parallel", "parallel"), 32),
        name="stick_breaking_attention",
    )(proj, proj, proj, proj)
    return oa, ob


MOE_TILE = 512
MOE_UP_COLS = 1024
MOE_DOWN_COLS = 512
DMA_ISSUE_UNROLL = 8
META_I1, META_I2, META_W1, META_W2, META_P1, META_P2 = range(6)


def _router_body(n_experts, x_ref, g_ref, rw_ref, h_ref, meta_ref, cnt_ref, carry_ref):
    tm = x_ref.shape[0]

    @pl.when(pl.program_id(0) == 0)
    def _():
        carry_ref[...] = jnp.zeros_like(carry_ref)

    h = _rms(x_ref[...], g_ref[...])
    h_ref[...] = h
    logits = jnp.dot(h, rw_ref[...], precision=lax.Precision.HIGHEST,
                     preferred_element_type=F32)
    lane = lax.broadcasted_iota(I32, logits.shape, 1)
    l1 = jnp.where(lane < n_experts, logits, NEG_INF)
    m1 = jnp.max(l1, axis=-1, keepdims=True)
    i1 = jnp.min(jnp.where(l1 == m1, lane, LANES), axis=-1, keepdims=True)
    l2 = jnp.where(lane == i1, NEG_INF, l1)
    m2 = jnp.max(l2, axis=-1, keepdims=True)
    i2 = jnp.min(jnp.where(l2 == m2, lane, LANES), axis=-1, keepdims=True)
    e2 = jnp.exp(m2 - m1)
    inv = 1.0 / (1.0 + e2)
    w1, w2 = inv, e2 * inv
    sel1, sel2 = lane == i1, lane == i2
    onehot = jnp.where(sel1 | sel2, 1.0, 0.0)
    r = lax.broadcasted_iota(I32, (tm, tm), 0)
    c = lax.broadcasted_iota(I32, (tm, tm), 1)
    before = (c < r).astype(BF16)
    pos = jnp.dot(before, onehot.astype(BF16), preferred_element_type=F32) + carry_ref[...]
    p1 = jnp.sum(jnp.where(sel1, pos, 0.0), axis=-1, keepdims=True)
    p2 = jnp.sum(jnp.where(sel2, pos, 0.0), axis=-1, keepdims=True)
    meta = jnp.zeros_like(logits)
    for idx, val in ((META_I1, i1.astype(F32)), (META_I2, i2.astype(F32)), (META_W1, w1),
                     (META_W2, w2), (META_P1, p1), (META_P2, p2)):
        meta = jnp.where(lane == idx, val, meta)
    meta_ref[...] = meta
    carry_ref[...] += jnp.sum(onehot, axis=0, keepdims=True)
    cnt_ref[...] = carry_ref[...]


def moe_router(x, g, rw, n_experts, tm=512):
    t, d = x.shape
    tm = min(tm, t)
    return pl.pallas_call(
        functools.partial(_router_body, n_experts),
        grid=(t // tm,),
        in_specs=[pl.BlockSpec((tm, d), lambda i: (i, 0)),
                  pl.BlockSpec((1, d), lambda i: (0, 0)),
                  pl.BlockSpec((d, LANES), lambda i: (0, 0))],
        out_specs=[pl.BlockSpec((tm, d), lambda i: (i, 0)),
                   pl.BlockSpec((tm, LANES), lambda i: (i, 0)),
                   pl.BlockSpec((1, LANES), lambda i: (0, 0))],
        out_shape=[jax.ShapeDtypeStruct((t, d), F32),
                   jax.ShapeDtypeStruct((t, LANES), F32),
                   jax.ShapeDtypeStruct((1, LANES), F32)],
        scratch_shapes=[pltpu.VMEM((1, LANES), F32)],
        compiler_params=_params(("arbitrary",), 32),
        name="moe_router",
    )(x, g, rw)


def _row_copy(src_hbm, src_row, dst_ref, dst_row, sem):
    return pltpu.make_async_copy(src_hbm.at[pl.ds(src_row, 1), :],
                                 dst_ref.at[pl.ds(dst_row, 1), :], sem)


def _moe_up_body(n_tiles, n_j, te_ref, nv_ref, tok_ref, h_hbm, wg_ref, wu_ref, o_ref,
                 x_ref, h_ref, sem):
    i, j = pl.program_id(0), pl.program_id(1)
    tm = h_ref.shape[0]
    n_valid = nv_ref[0]
    per_step = tm // n_j
    cur, nxt = i % 2, 1 - i % 2
    nxt_tile = jnp.minimum(i + 1, n_tiles - 1)

    def issue_rows(tile, slot, first, count):
        for u in range(count):
            _row_copy(h_hbm, tok_ref[tile * tm + first + u], x_ref.at[slot], first + u,
                      sem.at[slot]).start()

    def wait_tile(slot):
        pltpu.make_async_copy(h_hbm.at[pl.ds(0, tm), :], x_ref.at[slot], sem.at[slot]).wait()

    @pl.when((i == 0) & (j == 0))
    def _():
        def issue(r, carry):
            _row_copy(h_hbm, tok_ref[r], x_ref.at[0], r, sem.at[0]).start()
            return carry

        lax.fori_loop(0, tm, issue, 0, unroll=DMA_ISSUE_UNROLL)

    @pl.when(j == 0)
    def _():
        wait_tile(cur)
        h_ref[...] = x_ref[cur].astype(BF16)
        issue_rows(nxt_tile, nxt, per_step * n_j, tm - per_step * n_j)

    @pl.when(i < n_valid)
    def _():
        issue_rows(nxt_tile, nxt, j * per_step, per_step)
        h = h_ref[...]
        a = jnp.dot(h, wg_ref[...], preferred_element_type=F32)
        b = jnp.dot(h, wu_ref[...], preferred_element_type=F32)
        o_ref[...] = _silu_mul(a, b).astype(o_ref.dtype)

    @pl.when(i >= n_valid)
    def _():
        issue_rows(nxt_tile, nxt, j * per_step, per_step)
        o_ref[...] = jnp.zeros_like(o_ref)

    @pl.when((i == n_tiles - 1) & (j == n_j - 1))
    def _():
        wait_tile(nxt)


def _column_tiles(w, tn):
    e, k, n = w.shape
    tn = min(tn, n)
    return w.astype(BF16).reshape(e, k, n // tn, tn).transpose(0, 2, 1, 3)


def moe_up(h, tok, tile_expert, n_valid, wg, wu):
    rows = tok.shape[0]
    _, nj, d, tn = wg.shape
    f = nj * tn
    tm = MOE_TILE
    return pl.pallas_call(
        functools.partial(_moe_up_body, rows // tm, nj),
        grid_spec=pltpu.PrefetchScalarGridSpec(
            num_scalar_prefetch=3,
            grid=(rows // tm, nj),
            in_specs=[pl.BlockSpec(memory_space=pl.ANY),
                      pl.BlockSpec((None, None, d, tn), lambda i, j, te, nv, tok: (te[i], j, 0, 0)),
                      pl.BlockSpec((None, None, d, tn), lambda i, j, te, nv, tok: (te[i], j, 0, 0))],
            out_specs=pl.BlockSpec((tm, tn), lambda i, j, te, nv, tok: (i, j)),
            scratch_shapes=[pltpu.VMEM((2, tm, d), F32),
                            pltpu.VMEM((tm, d), BF16),
                            pltpu.SemaphoreType.DMA((2,))]),
        out_shape=jax.ShapeDtypeStruct((rows, f), BF16),
        compiler_params=_params(("arbitrary", "arbitrary"), 48),
        name="moe_up",
    )(tile_expert, n_valid, tok, h, wg, wu)


def _moe_down_body(te_ref, nv_ref, a_ref, w_ref, o_ref):
    i = pl.program_id(0)

    @pl.when(i < nv_ref[0])
    def _():
        o_ref[...] = jnp.dot(a_ref[...], w_ref[...], preferred_element_type=F32)

    @pl.when(i >= nv_ref[0])
    def _():
        o_ref[...] = jnp.zeros_like(o_ref)


def moe_down(act, tile_expert, n_valid, wd):
    rows, f = act.shape
    _, nj, _, tn = wd.shape
    d = nj * tn
    tm = MOE_TILE
    return pl.pallas_call(
        _moe_down_body,
        grid_spec=pltpu.PrefetchScalarGridSpec(
            num_scalar_prefetch=2,
            grid=(rows // tm, nj),
            in_specs=[pl.BlockSpec((tm, f), lambda i, j, te, nv: (i, 0)),
                      pl.BlockSpec((None, None, f, tn), lambda i, j, te, nv: (te[i], j, 0, 0))],
            out_specs=pl.BlockSpec((tm, tn), lambda i, j, te, nv: (i, j))),
        out_shape=jax.ShapeDtypeStruct((rows, d), F32),
        compiler_params=_params(("parallel", "arbitrary"), 56),
        name="moe_down",
    )(tile_expert, n_valid, act, wd)


def _combine_body(d1_ref, d2_ref, x_ref, meta_ref, g_ref, y_hbm, o_ref, buf_ref, sem):
    tc = x_ref.shape[0]
    base = pl.program_id(0) * tc

    def issue(r, carry):
        _row_copy(y_hbm, d1_ref[base + r], buf_ref.at[0], r, sem).start()
        _row_copy(y_hbm, d2_ref[base + r], buf_ref.at[1], r, sem).start()
        return carry

    lax.fori_loop(0, tc, issue, 0, unroll=DMA_ISSUE_UNROLL)
    for k in range(TOP_K):
        pltpu.make_async_copy(y_hbm.at[pl.ds(0, tc), :], buf_ref.at[k], sem).wait()
    meta = meta_ref[...]
    w1 = meta[:, META_W1:META_W1 + 1]
    w2 = meta[:, META_W2:META_W2 + 1]
    y = x_ref[...] + (w1 * buf_ref[0] + w2 * buf_ref[1])
    o_ref[...] = _rms(y, g_ref[...])


def moe_combine_norm(x, meta, d1, d2, y, g, tc=256):
    t, d = x.shape
    tc = min(tc, t)
    return pl.pallas_call(
        _combine_body,
        grid_spec=pltpu.PrefetchScalarGridSpec(
            num_scalar_prefetch=2,
            grid=(t // tc,),
            in_specs=[pl.BlockSpec((tc, d), lambda i, a, b: (i, 0)),
                      pl.BlockSpec((tc, LANES), lambda i, a, b: (i, 0)),
                      pl.BlockSpec((1, d), lambda i, a, b: (0, 0)),
                      pl.BlockSpec(memory_space=pl.ANY)],
            out_specs=pl.BlockSpec((tc, d), lambda i, a, b: (i, 0)),
            scratch_shapes=[pltpu.VMEM((2, tc, d), F32),
                            pltpu.SemaphoreType.DMA(())]),
        out_shape=jax.ShapeDtypeStruct((t, d), F32),
        compiler_params=_params(("arbitrary",), 32),
        name="moe_combine_norm",
    )(d1, d2, x, meta, g, y)


def moe_layer_and_final_norm(x, g, router_w, w_gate, w_up, w_down, final_g):
    t, d = x.shape
    n_experts = router_w.shape[1]
    tm = MOE_TILE
    rw = jnp.zeros((d, LANES), F32).at[:, :n_experts].set(router_w)
    h, meta, cnt = moe_router(x, g, rw, n_experts)

    rows = t * TOP_K + n_experts * tm
    n_tiles = rows // tm
    counts = cnt[0, :n_experts].astype(I32)
    tiles_per = (counts + tm - 1) // tm
    tile_end = jnp.cumsum(tiles_per)
    offs = (tile_end - tiles_per) * tm
    n_valid = tile_end[-1:]
    tile_expert = jnp.minimum(
        jnp.searchsorted(tile_end, jnp.arange(n_tiles, dtype=I32), side="right"),
        n_experts - 1).astype(I32)
    i1 = meta[:, META_I1].astype(I32)
    i2 = meta[:, META_I2].astype(I32)
    d1 = offs[i1] + meta[:, META_P1].astype(I32)
    d2 = offs[i2] + meta[:, META_P2].astype(I32)
    tid = jnp.arange(t, dtype=I32)
    tok = jnp.zeros((rows,), I32).at[jnp.concatenate([d1, d2])].set(jnp.concatenate([tid, tid]))

    act = moe_up(h, tok, tile_expert, n_valid, _column_tiles(w_gate, MOE_UP_COLS),
                 _column_tiles(w_up, MOE_UP_COLS))
    y = moe_down(act, tile_expert, n_valid, _column_tiles(w_down, MOE_DOWN_COLS))
    return moe_combine_norm(x, meta, d1, d2, y, final_g)


def kernel(x, rel_bias, attn_norm_g, w_in, mix_norm_dil_g, mix_norm_sb_g, w_o, ffn_norm_g,
           dense_w_gate, dense_w_up, dense_w_down, router_w, moe_w_gate, moe_w_up,
           moe_w_down, final_norm_g):
    b, s, d = x.shape
    depth = w_in.shape[0]
    assert depth == 2, "layer 0 dense SwiGLU, layer 1 routed experts followed by the final norm"
    d_dil = mix_norm_dil_g.shape[1]
    d_sb = mix_norm_sb_g.shape[1]
    h_dil, h_sb = d_dil // HEAD_DIM, d_sb // HEAD_DIM
    tables = _dilated_bias_tables(rel_bias)
    row = lambda v: v.reshape(1, -1)

    xt = x.reshape(b * s, d)
    for layer in range(depth):
        w_in_l = w_in[layer].astype(BF16)
        g_attn = row(attn_norm_g[layer])
        p_dil, p_sb = norm_matmul_split(xt, g_attn, w_in_l, 3 * d_dil, F32, BF16)
        o_dil = dilated_attention(p_dil.reshape(b, s, 3 * d_dil), tables, h_dil)
        o_sb = stick_breaking_attention(p_sb.reshape(b, s, 3 * d_sb), h_sb)
        xt = mix_out(o_dil.reshape(b * s, d_dil),
                     tuple(o.reshape(b * s // 2, d_sb) for o in o_sb),
                     row(mix_norm_dil_g[layer]), row(mix_norm_sb_g[layer]),
                     w_o[layer].astype(BF16), xt, s)
        g_ffn = row(ffn_norm_g[layer])
        j = layer // 2
        if layer % 2 == 0:
            act = ffn_up(xt, g_ffn, dense_w_gate[j].astype(BF16), dense_w_up[j].astype(BF16))
            xt = matmul_residual(act, dense_w_down[j].astype(BF16), xt)
        else:
            xt = moe_layer_and_final_norm(
                xt, g_ffn, router_w[j], moe_w_gate[j], moe_w_up[j], moe_w_down[j],
                row(final_norm_g))
    return xt.reshape(b, s, d)
```

```python
import functools

import numpy as np
import jax
import jax.numpy as jnp
from jax import lax
from jax.experimental import pallas as pl
from jax.experimental.pallas import tpu as pltpu

F32 = jnp.float32
BF16 = jnp.bfloat16
I32 = jnp.int32

EPS = 1e-6
HEAD_DIM = 128
DILATED_BRANCHES = ((128, 1), (512, 4), (2048, 16))
N_BUCKETS = 32
MAX_DISTANCE = 2048
TOP_K = 2
LANES = 128
WIN = 128
NEG_INF = float("-inf")
DIL_UNROLL = 8
MIB = 1024 * 1024


def _params(semantics, vmem_mib):
    return pltpu.CompilerParams(dimension_semantics=semantics,
                                vmem_limit_bytes=vmem_mib * MIB)


def _rms(x, g):
    return x * lax.rsqrt(jnp.mean(x * x, axis=-1, keepdims=True) + EPS) * g


def _norm_matmul_body(n_first, x_ref, g_ref, w_ref, oa_ref, ob_ref, h_ref):
    j = pl.program_id(1)

    @pl.when(j == 0)
    def _():
        h_ref[...] = _rms(x_ref[...], g_ref[...]).astype(BF16)

    y = jnp.dot(h_ref[...], w_ref[...], preferred_element_type=F32)

    @pl.when(j < n_first)
    def _():
        oa_ref[...] = y.astype(oa_ref.dtype)

    @pl.when(j >= n_first)
    def _():
        ob_ref[...] = y.astype(ob_ref.dtype)


def norm_matmul_split(x, g, w, n_a, dtype_a, dtype_b, tm=1024, tn=1024):
    t, d = x.shape
    n = w.shape[1]
    tm, tn = min(tm, t), min(tn, n_a, n - n_a)
    assert n_a % tn == 0 and (n - n_a) % tn == 0
    ja = n_a // tn
    return pl.pallas_call(
        functools.partial(_norm_matmul_body, ja),
        grid=(t // tm, n // tn),
        in_specs=[pl.BlockSpec((tm, d), lambda i, j: (i, 0)),
                  pl.BlockSpec((1, d), lambda i, j: (0, 0)),
                  pl.BlockSpec((d, tn), lambda i, j: (0, j))],
        out_specs=[pl.BlockSpec((tm, tn), lambda i, j: (i, jnp.minimum(j, ja - 1))),
                   pl.BlockSpec((tm, tn), lambda i, j: (i, jnp.maximum(j - ja, 0)))],
        out_shape=[jax.ShapeDtypeStruct((t, n_a), dtype_a),
                   jax.ShapeDtypeStruct((t, n - n_a), dtype_b)],
        scratch_shapes=[pltpu.VMEM((tm, d), BF16)],
        compiler_params=_params(("parallel", "arbitrary"), 56),
        name="norm_matmul",
    )(x, g, w)


def _silu_mul(a, b):
    return a * (1.0 / (1.0 + jnp.exp(-a))) * b


def _ffn_up_body(x_ref, g_ref, wg_ref, wu_ref, o_ref, h_ref):
    @pl.when(pl.program_id(1) == 0)
    def _():
        h_ref[...] = _rms(x_ref[...], g_ref[...]).astype(BF16)

    h = h_ref[...]
    a = jnp.dot(h, wg_ref[...], preferred_element_type=F32)
    b = jnp.dot(h, wu_ref[...], preferred_element_type=F32)
    o_ref[...] = _silu_mul(a, b).astype(o_ref.dtype)


def ffn_up(x, g, wg, wu, tm=1024, tn=512):
    t, d = x.shape
    f = wg.shape[1]
    tm, tn = min(tm, t), min(tn, f)
    return pl.pallas_call(
        _ffn_up_body,
        grid=(t // tm, pl.cdiv(f, tn)),
        in_specs=[pl.BlockSpec((tm, d), lambda i, j: (i, 0)),
                  pl.BlockSpec((1, d), lambda i, j: (0, 0)),
                  pl.BlockSpec((d, tn), lambda i, j: (0, j)),
                  pl.BlockSpec((d, tn), lambda i, j: (0, j))],
        out_specs=pl.BlockSpec((tm, tn), lambda i, j: (i, j)),
        out_shape=jax.ShapeDtypeStruct((t, f), BF16),
        scratch_shapes=[pltpu.VMEM((tm, d), BF16)],
        compiler_params=_params(("parallel", "arbitrary"), 56),
        name="ffn_up",
    )(x, g, wg, wu)


def _matmul_res_body(a_ref, w_ref, r_ref, o_ref):
    o_ref[...] = r_ref[...] + jnp.dot(a_ref[...], w_ref[...], preferred_element_type=F32)


def matmul_residual(a, w, res, tm=512, tn=1024):
    t, f = a.shape
    n = w.shape[1]
    tm, tn = min(tm, t), min(tn, n)
    return pl.pallas_call(
        _matmul_res_body,
        grid=(n // tn, t // tm),
        in_specs=[pl.BlockSpec((tm, f), lambda j, i: (i, 0)),
                  pl.BlockSpec((f, tn), lambda j, i: (0, j)),
                  pl.BlockSpec((tm, tn), lambda j, i: (i, j))],
        out_specs=pl.BlockSpec((tm, tn), lambda j, i: (i, j)),
        out_shape=jax.ShapeDtypeStruct((t, n), F32),
        compiler_params=_params(("parallel", "parallel"), 52),
        name="matmul_residual",
    )(a, w, res)


def _mix_out_body(tiles_per_seq, od_ref, sa_ref, sb_ref, gd_ref, gs_ref, w_ref, x_ref, o_ref, h_ref):
    dd = od_ref.shape[1]
    half = tiles_per_seq // 2
    si = pl.program_id(0) % tiles_per_seq

    @pl.when(pl.program_id(1) == 0)
    def _():
        h_ref[:, :dd] = _rms(od_ref[...].astype(F32), gd_ref[...]).astype(BF16)

        @pl.when(si < half)
        def _():
            h_ref[:, dd:] = _rms(sa_ref[...].astype(F32), gs_ref[...]).astype(BF16)

        @pl.when(si >= half)
        def _():
            h_ref[:, dd:] = _rms(sb_ref[...].astype(F32), gs_ref[...]).astype(BF16)

    o_ref[...] = x_ref[...] + jnp.dot(h_ref[...], w_ref[...], preferred_element_type=F32)


def mix_out(o_dil, o_sb_halves, g_dil, g_sb, w_o, x, seq, tm=1024, tn=1024):
    t, dd = o_dil.shape
    sa, sb = o_sb_halves
    ds = sa.shape[1]
    n = w_o.shape[1]
    tm, tn = min(tm, seq // 2), min(tn, n)
    tps = seq // tm
    half = tps // 2
    return pl.pallas_call(
        functools.partial(_mix_out_body, tps),
        grid=(t // tm, n // tn),
        in_specs=[pl.BlockSpec((tm, dd), lambda i, j: (i, 0)),
                  pl.BlockSpec((tm, ds), lambda i, j: ((i // tps) * half + jnp.minimum(i % tps, half - 1), 0)),
                  pl.BlockSpec((tm, ds), lambda i, j: ((i // tps) * half + jnp.maximum(i % tps - half, 0), 0)),
                  pl.BlockSpec((1, dd), lambda i, j: (0, 0)),
                  pl.BlockSpec((1, ds), lambda i, j: (0, 0)),
                  pl.BlockSpec((dd + ds, tn), lambda i, j: (0, j)),
                  pl.BlockSpec((tm, tn), lambda i, j: (i, j))],
        out_specs=pl.BlockSpec((tm, tn), lambda i, j: (i, j)),
        out_shape=jax.ShapeDtypeStruct((t, n), F32),
        scratch_shapes=[pltpu.VMEM((tm, dd + ds), BF16)],
        compiler_params=_params(("parallel", "arbitrary"), 56),
        name="mix_out",
    )(o_dil, sa, sb, g_dil, g_sb, w_o, x)


def _t5_bucket(dist):
    n = np.asarray(dist, dtype=np.int64)
    max_exact = N_BUCKETS // 2
    large = max_exact + (np.log(np.maximum(n, 1) / max_exact)
                         / np.log(MAX_DISTANCE / max_exact)
                         * (N_BUCKETS - max_exact)).astype(np.int64)
    large = np.minimum(large, N_BUCKETS - 1)
    return np.where(n < max_exact, n, large).astype(np.int32)


def _dilated_bias_tables(rel_bias):
    n_heads = rel_bias.shape[1]
    period = 3 * WIN
    pad = jnp.full((n_heads, period - WIN - 1), NEG_INF, F32)
    tables = []
    for (w, d) in DILATED_BRANCHES:
        assert w // d == WIN
        bias_m = rel_bias[_t5_bucket(d * np.arange(WIN, -1, -1))].astype(F32).T
        u = jnp.concatenate([bias_m, pad], axis=1)
        shifted = jnp.tile(u, (1, WIN))[:, :WIN * (period - 1)].reshape(n_heads, WIN, period - 1)
        tables.append(shifted[:, :, :2 * WIN])
    return jnp.stack(tables, axis=1)


def _dilated_body(q_ref, k_ref, v_ref, tb_ref, o_ref, ob_ref, lb_ref):
    s = q_ref.shape[0]
    scale = HEAD_DIM ** -0.5

    def load(ref, pos, d):
        return ref[pl.ds(pos, WIN, stride=d), :]

    def attend(bi, d, base, q, kk, vv, tb):
        sc = lax.dot_general(q, kk, (((1,), (1,)), ((), ())), preferred_element_type=F32) + tb
        m = jnp.max(sc, axis=-1, keepdims=True)
        p = jnp.exp(sc - m)
        l = jnp.sum(p, axis=-1, keepdims=True)
        out = jnp.dot(p.astype(BF16), vv, preferred_element_type=F32) * (1.0 / l)
        lse = m + jnp.log(l)
        ob_ref[bi, pl.ds(base, WIN, stride=d), :] = out
        lb_ref[bi, pl.ds(base, WIN, stride=d), :] = jnp.broadcast_to(lse, (WIN, LANES))

    def run(bi, d, start, nb, k_prev, v_prev):
        for b in range(nb):
            base = start + d * WIN * b
            q = (load(q_ref, base, d) * scale).astype(BF16)
            k_cur = load(k_ref, base, d).astype(BF16)
            v_cur = load(v_ref, base, d).astype(BF16)
            if k_prev is None:
                attend(bi, d, base, q, k_cur, v_cur, tb_ref[bi, :, WIN:])
            else:
                attend(bi, d, base, q, jnp.concatenate([k_prev, k_cur], axis=0),
                       jnp.concatenate([v_prev, v_cur], axis=0), tb_ref[bi])
            k_prev, v_prev = k_cur, v_cur

    for bi, (_, d) in enumerate(DILATED_BRANCHES):
        nblk = s // (d * WIN)
        if nblk <= DIL_UNROLL:
            per = DIL_UNROLL // nblk

            def trip(g, carry, bi=bi, d=d, nblk=nblk, per=per):
                for u in range(per):
                    run(bi, d, g * per + u, nblk, None, None)
                return carry

            lax.fori_loop(0, d // per, trip, 0)
        else:
            chunks = nblk // DIL_UNROLL
            span = d * WIN * DIL_UNROLL

            def head(r, carry, bi=bi, d=d):
                run(bi, d, r, DIL_UNROLL, None, None)
                return carry

            def tail(it, carry, bi=bi, d=d, chunks=chunks, span=span):
                start = it // (chunks - 1) + span * (it % (chunks - 1) + 1)
                run(bi, d, start, DIL_UNROLL, load(k_ref, start - d * WIN, d).astype(BF16),
                    load(v_ref, start - d * WIN, d).astype(BF16))
                return carry

            lax.fori_loop(0, d, head, 0)
            lax.fori_loop(0, d * (chunks - 1), tail, 0)

    ch = 256
    def merge(c, carry):
        rows = pl.ds(pl.multiple_of(c * ch, ch), ch)
        ls = [lb_ref[b, rows, :] for b in range(len(DILATED_BRANCHES))]
        mx = functools.reduce(jnp.maximum, ls)
        es = [jnp.exp(x - mx) for x in ls]
        inv = 1.0 / functools.reduce(jnp.add, es)
        acc = es[0] * inv * ob_ref[0, rows, :]
        for b in range(1, len(DILATED_BRANCHES)):
            acc = acc + es[b] * inv * ob_ref[b, rows, :]
        o_ref[rows, :] = acc.astype(o_ref.dtype)
        return carry

    lax.fori_loop(0, s // ch, merge, 0)


def dilated_attention(proj, tables, n_heads):
    b, s, _ = proj.shape
    h = n_heads
    nbr = len(DILATED_BRANCHES)
    return pl.pallas_call(
        _dilated_body,
        grid=(b, h),
        in_specs=[pl.BlockSpec((None, s, HEAD_DIM), lambda bi, hi: (bi, 0, hi)),
                  pl.BlockSpec((None, s, HEAD_DIM), lambda bi, hi: (bi, 0, h + hi)),
                  pl.BlockSpec((None, s, HEAD_DIM), lambda bi, hi: (bi, 0, 2 * h + hi)),
                  pl.BlockSpec((None, nbr, WIN, 2 * WIN), lambda bi, hi: (hi, 0, 0, 0))],
        out_specs=pl.BlockSpec((None, s, HEAD_DIM), lambda bi, hi: (bi, 0, hi)),
        out_shape=jax.ShapeDtypeStruct((b, s, h * HEAD_DIM), BF16),
        scratch_shapes=[pltpu.VMEM((nbr, s, HEAD_DIM), F32),
                        pltpu.VMEM((nbr, s, LANES), F32)],
        compiler_params=_params(("parallel", "parallel"), 48),
        name="dilated_attention",
    )(proj, proj, proj, tables)


SB_BLOCK = 256
LOG2E = 1.4426950408889634


def _softplus2(z2):
    return jnp.maximum(z2, 0.0) + jnp.log2(1.0 + jnp.exp2(-jnp.abs(z2)))


def _sb_body(nq, qa_ref, qb_ref, k_ref, v_ref, oa_ref, ob_ref, q_ref, acc_ref, c_ref):
    blk = SB_BLOCK
    p = pl.program_id(2)
    row = lax.broadcasted_iota(I32, (blk, blk), 0)
    col = lax.broadcasted_iota(I32, (blk, blk), 1)
    tri = (row >= col).astype(BF16)
    causal = col < row
    for slot, src in enumerate((qa_ref, qb_ref)):
        q_ref[slot] = (src[...].astype(F32) * (HEAD_DIM ** -0.5 * LOG2E)).astype(BF16)

    work = [(0, p, True), (1, nq - 1 - p, True)]
    for t in range(nq - 1):
        first = t < p
        work.append((jnp.where(first, 0, 1), jnp.where(first, p - 1 - t, nq - 2 - t), False))
    zs, vs, sums, cs = {}, {}, {}, {}

    def stage_scores(i):
        slot, kb, _ = work[i]
        rows = pl.ds(pl.multiple_of(kb * blk, blk), blk)
        zs[i] = lax.dot_general(q_ref[slot], k_ref[rows, :], (((1,), (1,)), ((), ())),
                                preferred_element_type=F32)
        vs[i] = v_ref[rows, :]

    def stage_sums(i):
        slot, _, diag = work[i]
        sp = _softplus2(zs[i])
        if diag:
            sp = jnp.where(causal, sp, 0.0)
        total = jnp.broadcast_to(jnp.sum(sp, axis=-1, keepdims=True), (blk, LANES))
        if diag:
            c_ref[slot] = total
        else:
            cs[i] = c_ref[slot]
            c_ref[slot] = cs[i] + total
        sums[i] = jnp.dot(sp.astype(BF16), tri, preferred_element_type=F32)

    def stage_out(i):
        slot, _, diag = work[i]
        e = zs.pop(i) - sums.pop(i)
        if diag:
            a = jnp.where(causal, jnp.exp2(e), 0.0)
        else:
            c = cs.pop(i)
            a = jnp.exp2(e - jnp.concatenate([c, c], axis=1))
        pv = jnp.dot(a.astype(BF16), vs.pop(i), preferred_element_type=F32)
        if diag:
            acc_ref[slot] = pv
        else:
            acc_ref[slot] += pv

    n = len(work)
    for step in range(n + 2):
        if step < n:
            stage_scores(step)
        if 0 <= step - 1 < n:
            stage_sums(step - 1)
        if 0 <= step - 2 < n:
            stage_out(step - 2)
    oa_ref[...] = acc_ref[0].astype(oa_ref.dtype)
    ob_ref[...] = acc_ref[1].astype(ob_ref.dtype)


def stick_breaking_attention(proj, n_heads):
    b, s, _ = proj.shape
    h = n_heads
    blk = SB_BLOCK
    nq = s // blk
    assert LANES * 2 == blk and nq % 2 == 0
    q_spec = lambda f: pl.BlockSpec((None, blk, HEAD_DIM), lambda bi, hi, p: (bi, f(p), hi))
    oa, ob = pl.pallas_call(
        functools.partial(_sb_body, nq),
        grid=(b, h, nq // 2),
        in_specs=[q_spec(lambda p: p), q_spec(lambda p: nq - 1 - p),
                  pl.BlockSpec((None, s, HEAD_DIM), lambda bi, hi, p: (bi, 0, h + hi)),
                  pl.BlockSpec((None, s, HEAD_DIM), lambda bi, hi, p: (bi, 0, 2 * h + hi))],
        out_specs=[q_spec(lambda p: p), q_spec(lambda p: nq // 2 - 1 - p)],
        out_shape=[jax.ShapeDtypeStruct((b, s // 2, h * HEAD_DIM), BF16)] * 2,
        scratch_shapes=[pltpu.VMEM((2, blk, HEAD_DIM), BF16),
                        pltpu.VMEM((2, blk, HEAD_DIM), F32),
                        pltpu.VMEM((2, blk, LANES), F32)],
        compiler_params=_params(("parallel", "parallel", "parallel"), 32),
        name="stick_breaking_attention",
    )(proj, proj, proj, proj)
    return oa, ob


MOE_TILE = 512
DMA_ISSUE_UNROLL = 8
W_STREAMS = 2
META_I1, META_I2, META_W1, META_W2, META_P1, META_P2 = range(6)


def _router_body(n_experts, x_ref, g_ref, rw_ref, h_ref, meta_ref, cnt_ref, carry_ref):
    tm = x_ref.shape[0]

    @pl.when(pl.program_id(0) == 0)
    def _():
        carry_ref[...] = jnp.zeros_like(carry_ref)

    h = _rms(x_ref[...], g_ref[...])
    h_ref[...] = h
    rw = rw_ref[...]
    h_hi, rw_hi = h.astype(BF16), rw.astype(BF16)
    h_lo = (h - h_hi.astype(F32)).astype(BF16)
    rw_lo = (rw - rw_hi.astype(F32)).astype(BF16)
    logits = (jnp.dot(h_hi, rw_hi, preferred_element_type=F32)
              + (jnp.dot(h_lo, rw_hi, preferred_element_type=F32)
                 + jnp.dot(h_hi, rw_lo, preferred_element_type=F32)))
    lane = lax.broadcasted_iota(I32, logits.shape, 1)
    l1 = jnp.where(lane < n_experts, logits, NEG_INF)
    m1 = jnp.max(l1, axis=-1, keepdims=True)
    i1 = jnp.min(jnp.where(l1 == m1, lane, LANES), axis=-1, keepdims=True)
    l2 = jnp.where(lane == i1, NEG_INF, l1)
    m2 = jnp.max(l2, axis=-1, keepdims=True)
    i2 = jnp.min(jnp.where(l2 == m2, lane, LANES), axis=-1, keepdims=True)
    e2 = jnp.exp(m2 - m1)
    inv = 1.0 / (1.0 + e2)
    w1, w2 = inv, e2 * inv
    sel1, sel2 = lane == i1, lane == i2
    onehot = jnp.where(sel1 | sel2, 1.0, 0.0)
    r = lax.broadcasted_iota(I32, (tm, tm), 0)
    c = lax.broadcasted_iota(I32, (tm, tm), 1)
    before = (c < r).astype(BF16)
    pos = jnp.dot(before, onehot.astype(BF16), preferred_element_type=F32) + carry_ref[...]
    p1 = jnp.sum(jnp.where(sel1, pos, 0.0), axis=-1, keepdims=True)
    p2 = jnp.sum(jnp.where(sel2, pos, 0.0), axis=-1, keepdims=True)
    meta = jnp.zeros_like(logits)
    for idx, val in ((META_I1, i1.astype(F32)), (META_I2, i2.astype(F32)), (META_W1, w1),
                     (META_W2, w2), (META_P1, p1), (META_P2, p2)):
        meta = jnp.where(lane == idx, val, meta)
    meta_ref[...] = meta
    carry_ref[...] += jnp.sum(onehot, axis=0, keepdims=True)
    cnt_ref[...] = carry_ref[...]


def moe_router(x, g, rw, n_experts, tm=512):
    t, d = x.shape
    tm = min(tm, t)
    return pl.pallas_call(
        functools.partial(_router_body, n_experts),
        grid=(t // tm,),
        in_specs=[pl.BlockSpec((tm, d), lambda i: (i, 0)),
                  pl.BlockSpec((1, d), lambda i: (0, 0)),
                  pl.BlockSpec((d, LANES), lambda i: (0, 0))],
        out_specs=[pl.BlockSpec((tm, d), lambda i: (i, 0)),
                   pl.BlockSpec((tm, LANES), lambda i: (i, 0)),
                   pl.BlockSpec((1, LANES), lambda i: (0, 0))],
        out_shape=[jax.ShapeDtypeStruct((t, d), F32),
                   jax.ShapeDtypeStruct((t, LANES), F32),
                   jax.ShapeDtypeStruct((1, LANES), F32)],
        scratch_shapes=[pltpu.VMEM((1, LANES), F32)],
        compiler_params=_params(("arbitrary",), 32),
        name="moe_router",
    )(x, g, rw)


def _row_copy(src_hbm, src_row, dst_ref, dst_row, sem):
    return pltpu.make_async_copy(src_hbm.at[pl.ds(src_row, 1), :],
                                 dst_ref.at[pl.ds(dst_row, 1), :], sem)


def _moe_up_body(te_ref, nv_ref, tok_ref, h_hbm, *refs):
    w_refs = refs[:2 * W_STREAMS]
    o_ref, x_ref, h_ref, sem = refs[2 * W_STREAMS:]
    i, j = pl.program_id(0), pl.program_id(1)
    tm = h_ref.shape[0]
    n_valid = nv_ref[0]

    def start_gather(tile, slot):
        base = tile * tm

        def issue(r, carry):
            _row_copy(h_hbm, tok_ref[base + r], x_ref.at[slot], r, sem.at[slot]).start()
            return carry

        lax.fori_loop(0, tm, issue, 0, unroll=DMA_ISSUE_UNROLL)

    @pl.when((i == 0) & (j == 0))
    def _():
        start_gather(0, 0)

    @pl.when(i < n_valid)
    def _():
        @pl.when(j == 0)
        def _():
            slot = i % 2
            pltpu.make_async_copy(h_hbm.at[pl.ds(0, tm), :], x_ref.at[slot], sem.at[slot]).wait()
            h_ref[...] = x_ref[slot].astype(BF16)

            @pl.when(i + 1 < n_valid)
            def _():
                start_gather(i + 1, 1 - slot)

        a = _split_k_dot(h_ref, w_refs[:W_STREAMS])
        b = _split_k_dot(h_ref, w_refs[W_STREAMS:])
        o_ref[...] = _silu_mul(a, b).astype(o_ref.dtype)

    @pl.when(i >= nv_ref[0])
    def _():
        o_ref[...] = jnp.zeros_like(o_ref)


def moe_up(h, tok, tile_expert, n_valid, wg, wu, tn=1792):
    d = h.shape[1]
    rows = tok.shape[0]
    f = wg.shape[2]
    tm, tn = MOE_TILE, min(tn, f)
    return pl.pallas_call(
        _moe_up_body,
        grid_spec=pltpu.PrefetchScalarGridSpec(
            num_scalar_prefetch=3,
            grid=(rows // tm, f // tn),
            in_specs=[pl.BlockSpec(memory_space=pl.ANY)]
            + 2 * _row_chunk_specs(
                d, tn, lambda i, j, te, nv, tok: (te[i], _hold_after(i, nv, j, f // tn - 1))),
            out_specs=pl.BlockSpec((tm, tn), lambda i, j, te, nv, tok: (i, j)),
            scratch_shapes=[pltpu.VMEM((2, tm, d), F32),
                            pltpu.VMEM((tm, d), BF16),
                            pltpu.SemaphoreType.DMA((2,))]),
        out_shape=jax.ShapeDtypeStruct((rows, f), BF16),
        compiler_params=_params(("arbitrary", "arbitrary"), 56),
        name="moe_up",
    )(tile_expert, n_valid, tok, h, *([wg] * W_STREAMS + [wu] * W_STREAMS))


def _hold_after(i, nv_ref, j, last_j):
    return jnp.where(i < nv_ref[0], j, last_j)


def _split_k_dot(x_ref, w_refs):
    kc = w_refs[0].shape[0]
    acc = jnp.dot(x_ref[:, :kc], w_refs[0][...], preferred_element_type=F32)
    for s in range(1, len(w_refs)):
        acc += jnp.dot(x_ref[:, s * kc:(s + 1) * kc], w_refs[s][...], preferred_element_type=F32)
    return acc


def _row_chunk_specs(k, tn, index):
    kc = k // W_STREAMS

    def spec(s):
        def index_map(*args):
            e, j = index(*args)
            return (e, s, j)
        return pl.BlockSpec((None, kc, tn), index_map)

    return [spec(s) for s in range(W_STREAMS)]


def _moe_down_body(te_ref, nv_ref, a_ref, *refs):
    w_refs, o_ref = refs[:-1], refs[-1]
    i = pl.program_id(0)

    @pl.when(i < nv_ref[0])
    def _():
        o_ref[...] = _split_k_dot(a_ref, w_refs)

    @pl.when(i >= nv_ref[0])
    def _():
        o_ref[...] = jnp.zeros_like(o_ref)


def moe_down(act, tile_expert, n_valid, wd, tn=512):
    rows, f = act.shape
    d = wd.shape[2]
    tm, tn = MOE_TILE, min(tn, d)
    return pl.pallas_call(
        _moe_down_body,
        grid_spec=pltpu.PrefetchScalarGridSpec(
            num_scalar_prefetch=2,
            grid=(rows // tm, d // tn),
            in_specs=[pl.BlockSpec((tm, f), lambda i, j, te, nv: (jnp.minimum(i, nv[0] - 1), 0))]
            + _row_chunk_specs(
                f, tn, lambda i, j, te, nv: (te[i], _hold_after(i, nv, j, d // tn - 1))),
            out_specs=pl.BlockSpec((tm, tn), lambda i, j, te, nv: (i, j))),
        out_shape=jax.ShapeDtypeStruct((rows, d), F32),
        compiler_params=_params(("parallel", "arbitrary"), 56),
        name="moe_down",
    )(tile_expert, n_valid, act, *([wd] * W_STREAMS))


def _combine_body(d1_ref, d2_ref, x_ref, meta_ref, g_ref, y_hbm, o_ref, buf_ref, sem):
    tc = x_ref.shape[0]
    base = pl.program_id(0) * tc

    def issue(r, carry):
        _row_copy(y_hbm, d1_ref[base + r], buf_ref.at[0], r, sem).start()
        _row_copy(y_hbm, d2_ref[base + r], buf_ref.at[1], r, sem).start()
        return carry

    lax.fori_loop(0, tc, issue, 0, unroll=DMA_ISSUE_UNROLL)
    for k in range(TOP_K):
        pltpu.make_async_copy(y_hbm.at[pl.ds(0, tc), :], buf_ref.at[k], sem).wait()
    meta = meta_ref[...]
    w1 = meta[:, META_W1:META_W1 + 1]
    w2 = meta[:, META_W2:META_W2 + 1]
    y = x_ref[...] + (w1 * buf_ref[0] + w2 * buf_ref[1])
    o_ref[...] = _rms(y, g_ref[...])


def moe_combine_norm(x, meta, d1, d2, y, g, tc=256):
    t, d = x.shape
    tc = min(tc, t)
    return pl.pallas_call(
        _combine_body,
        grid_spec=pltpu.PrefetchScalarGridSpec(
            num_scalar_prefetch=2,
            grid=(t // tc,),
            in_specs=[pl.BlockSpec((tc, d), lambda i, a, b: (i, 0)),
                      pl.BlockSpec((tc, LANES), lambda i, a, b: (i, 0)),
                      pl.BlockSpec((1, d), lambda i, a, b: (0, 0)),
                      pl.BlockSpec(memory_space=pl.ANY)],
            out_specs=pl.BlockSpec((tc, d), lambda i, a, b: (i, 0)),
            scratch_shapes=[pltpu.VMEM((2, tc, d), F32),
                            pltpu.SemaphoreType.DMA(())]),
        out_shape=jax.ShapeDtypeStruct((t, d), F32),
        compiler_params=_params(("arbitrary",), 32),
        name="moe_combine_norm",
    )(d1, d2, x, meta, g, y)


def moe_layer_and_final_norm(x, g, router_w, w_gate, w_up, w_down, final_g):
    t, d = x.shape
    n_experts = router_w.shape[1]
    tm = MOE_TILE
    rw = jnp.zeros((d, LANES), F32).at[:, :n_experts].set(router_w)
    h, meta, cnt = moe_router(x, g, rw, n_experts)

    rows = t * TOP_K + n_experts * tm
    n_tiles = rows // tm
    counts = cnt[0, :n_experts].astype(I32)
    tiles_per = (counts + tm - 1) // tm
    tile_end = jnp.cumsum(tiles_per)
    offs = (tile_end - tiles_per) * tm
    n_valid = tile_end[-1:]
    tile_expert = jnp.searchsorted(
        tile_end, jnp.minimum(jnp.arange(n_tiles, dtype=I32), n_valid - 1), side="right").astype(I32)
    i1 = meta[:, META_I1].astype(I32)
    i2 = meta[:, META_I2].astype(I32)
    d1 = offs[i1] + meta[:, META_P1].astype(I32)
    d2 = offs[i2] + meta[:, META_P2].astype(I32)
    tid = jnp.arange(t, dtype=I32)
    tok = jnp.zeros((rows,), I32).at[jnp.concatenate([d1, d2])].set(jnp.concatenate([tid, tid]))

    act = moe_up(h, tok, tile_expert, n_valid, w_gate, w_up)
    y = moe_down(act, tile_expert, n_valid, w_down)
    return moe_combine_norm(x, meta, d1, d2, y, final_g)


def kernel(x, rel_bias, attn_norm_g, w_in, mix_norm_dil_g, mix_norm_sb_g, w_o, ffn_norm_g,
           dense_w_gate, dense_w_up, dense_w_down, router_w, moe_w_gate, moe_w_up,
           moe_w_down, final_norm_g):
    b, s, d = x.shape
    depth = w_in.shape[0]
    assert depth == 2, "layer 0 dense SwiGLU, layer 1 routed experts followed by the final norm"
    d_dil = mix_norm_dil_g.shape[1]
    d_sb = mix_norm_sb_g.shape[1]
    h_dil, h_sb = d_dil // HEAD_DIM, d_sb // HEAD_DIM
    tables = _dilated_bias_tables(rel_bias)
    row = lambda v: v.reshape(1, -1)

    xt = x.reshape(b * s, d)
    for layer in range(depth):
        w_in_l = w_in[layer].astype(BF16)
        g_attn = row(attn_norm_g[layer])
        p_dil, p_sb = norm_matmul_split(xt, g_attn, w_in_l, 3 * d_dil, F32, BF16)
        o_dil = dilated_attention(p_dil.reshape(b, s, 3 * d_dil), tables, h_dil)
        o_sb = stick_breaking_attention(p_sb.reshape(b, s, 3 * d_sb), h_sb)
        xt = mix_out(o_dil.reshape(b * s, d_dil),
                     tuple(o.reshape(b * s // 2, d_sb) for o in o_sb),
                     row(mix_norm_dil_g[layer]), row(mix_norm_sb_g[layer]),
                     w_o[layer].astype(BF16), xt, s)
        g_ffn = row(ffn_norm_g[layer])
        j = layer // 2
        if layer % 2 == 0:
            act = ffn_up(xt, g_ffn, dense_w_gate[j].astype(BF16), dense_w_up[j].astype(BF16))
            xt = matmul_residual(act, dense_w_down[j].astype(BF16), xt)
        else:
            xt = moe_layer_and_final_norm(
                xt, g_ffn, router_w[j], moe_w_gate[j].astype(BF16), moe_w_up[j].astype(BF16),
                moe_w_down[j].astype(BF16), row(final_norm_g))
    return xt.reshape(b, s, d)
```

```python
import functools

import numpy as np
import jax
import jax.numpy as jnp
from jax import lax
from jax.experimental import pallas as pl
from jax.experimental.pallas import tpu as pltpu

F32 = jnp.float32
BF16 = jnp.bfloat16
I32 = jnp.int32

EPS = 1e-6
HEAD_DIM = 128
DILATED_BRANCHES = ((128, 1), (512, 4), (2048, 16))
N_BUCKETS = 32
MAX_DISTANCE = 2048
TOP_K = 2
LANES = 128
WIN = 128
NEG_INF = float("-inf")
DIL_UNROLL = 8
MIB = 1024 * 1024


def _params(semantics, vmem_mib):
    return pltpu.CompilerParams(dimension_semantics=semantics,
                                vmem_limit_bytes=vmem_mib * MIB)


def _rms(x, g):
    return x * lax.rsqrt(jnp.mean(x * x, axis=-1, keepdims=True) + EPS) * g


def _norm_matmul_body(n_first, x_ref, g_ref, w_ref, oa_ref, ob_ref, h_ref):
    j = pl.program_id(1)

    @pl.when(j == 0)
    def _():
        h_ref[...] = _rms(x_ref[...], g_ref[...]).astype(BF16)

    y = jnp.dot(h_ref[...], w_ref[...], preferred_element_type=F32)

    @pl.when(j < n_first)
    def _():
        oa_ref[...] = y.astype(oa_ref.dtype)

    @pl.when(j >= n_first)
    def _():
        ob_ref[...] = y.astype(ob_ref.dtype)


def norm_matmul_split(x, g, w, n_a, dtype_a, dtype_b, tm=1024, tn=1024):
    t, d = x.shape
    n = w.shape[1]
    tm, tn = min(tm, t), min(tn, n_a, n - n_a)
    assert n_a % tn == 0 and (n - n_a) % tn == 0
    ja = n_a // tn
    return pl.pallas_call(
        functools.partial(_norm_matmul_body, ja),
        grid=(t // tm, n // tn),
        in_specs=[pl.BlockSpec((tm, d), lambda i, j: (i, 0)),
                  pl.BlockSpec((1, d), lambda i, j: (0, 0)),
                  pl.BlockSpec((d, tn), lambda i, j: (0, j))],
        out_specs=[pl.BlockSpec((tm, tn), lambda i, j: (i, jnp.minimum(j, ja - 1))),
                   pl.BlockSpec((tm, tn), lambda i, j: (i, jnp.maximum(j - ja, 0)))],
        out_shape=[jax.ShapeDtypeStruct((t, n_a), dtype_a),
                   jax.ShapeDtypeStruct((t, n - n_a), dtype_b)],
        scratch_shapes=[pltpu.VMEM((tm, d), BF16)],
        compiler_params=_params(("parallel", "arbitrary"), 56),
        name="norm_matmul",
    )(x, g, w)


def _silu_mul(a, b):
    return a * (1.0 / (1.0 + jnp.exp(-a))) * b


def _ffn_up_body(x_ref, g_ref, wg_ref, wu_ref, o_ref, h_ref):
    @pl.when(pl.program_id(1) == 0)
    def _():
        h_ref[...] = _rms(x_ref[...], g_ref[...]).astype(BF16)

    h = h_ref[...]
    a = jnp.dot(h, wg_ref[...], preferred_element_type=F32)
    b = jnp.dot(h, wu_ref[...], preferred_element_type=F32)
    o_ref[...] = _silu_mul(a, b).astype(o_ref.dtype)


def ffn_up(x, g, wg, wu, tm=1024, tn=512):
    t, d = x.shape
    f = wg.shape[1]
    tm, tn = min(tm, t), min(tn, f)
    return pl.pallas_call(
        _ffn_up_body,
        grid=(t // tm, pl.cdiv(f, tn)),
        in_specs=[pl.BlockSpec((tm, d), lambda i, j: (i, 0)),
                  pl.BlockSpec((1, d), lambda i, j: (0, 0)),
                  pl.BlockSpec((d, tn), lambda i, j: (0, j)),
                  pl.BlockSpec((d, tn), lambda i, j: (0, j))],
        out_specs=pl.BlockSpec((tm, tn), lambda i, j: (i, j)),
        out_shape=jax.ShapeDtypeStruct((t, f), BF16),
        scratch_shapes=[pltpu.VMEM((tm, d), BF16)],
        compiler_params=_params(("parallel", "arbitrary"), 56),
        name="ffn_up",
    )(x, g, wg, wu)


def _matmul_res_body(a_ref, w_ref, r_ref, o_ref):
    o_ref[...] = r_ref[...] + jnp.dot(a_ref[...], w_ref[...], preferred_element_type=F32)


def matmul_residual(a, w, res, tm=512, tn=1024):
    t, f = a.shape
    n = w.shape[1]
    tm, tn = min(tm, t), min(tn, n)
    return pl.pallas_call(
        _matmul_res_body,
        grid=(n // tn, t // tm),
        in_specs=[pl.BlockSpec((tm, f), lambda j, i: (i, 0)),
                  pl.BlockSpec((f, tn), lambda j, i: (0, j)),
                  pl.BlockSpec((tm, tn), lambda j, i: (i, j))],
        out_specs=pl.BlockSpec((tm, tn), lambda j, i: (i, j)),
        out_shape=jax.ShapeDtypeStruct((t, n), F32),
        compiler_params=_params(("parallel", "parallel"), 52),
        name="matmul_residual",
    )(a, w, res)


def _mix_out_body(tiles_per_seq, od_ref, sa_ref, sb_ref, gd_ref, gs_ref, w_ref, x_ref, o_ref, h_ref):
    dd = od_ref.shape[1]
    half = tiles_per_seq // 2
    si = pl.program_id(0) % tiles_per_seq

    @pl.when(pl.program_id(1) == 0)
    def _():
        h_ref[:, :dd] = _rms(od_ref[...].astype(F32), gd_ref[...]).astype(BF16)

        @pl.when(si < half)
        def _():
            h_ref[:, dd:] = _rms(sa_ref[...].astype(F32), gs_ref[...]).astype(BF16)

        @pl.when(si >= half)
        def _():
            h_ref[:, dd:] = _rms(sb_ref[...].astype(F32), gs_ref[...]).astype(BF16)

    o_ref[...] = x_ref[...] + jnp.dot(h_ref[...], w_ref[...], preferred_element_type=F32)


def mix_out(o_dil, o_sb_halves, g_dil, g_sb, w_o, x, seq, tm=512, tn=2048):
    t, dd = o_dil.shape
    sa, sb = o_sb_halves
    ds = sa.shape[1]
    n = w_o.shape[1]
    tm, tn = min(tm, seq // 2), min(tn, n)
    tps = seq // tm
    half = tps // 2
    return pl.pallas_call(
        functools.partial(_mix_out_body, tps),
        grid=(t // tm, n // tn),
        in_specs=[pl.BlockSpec((tm, dd), lambda i, j: (i, 0)),
                  pl.BlockSpec((tm, ds), lambda i, j: ((i // tps) * half + jnp.minimum(i % tps, half - 1), 0)),
                  pl.BlockSpec((tm, ds), lambda i, j: ((i // tps) * half + jnp.maximum(i % tps - half, 0), 0)),
                  pl.BlockSpec((1, dd), lambda i, j: (0, 0)),
                  pl.BlockSpec((1, ds), lambda i, j: (0, 0)),
                  pl.BlockSpec((dd + ds, tn), lambda i, j: (0, j)),
                  pl.BlockSpec((tm, tn), lambda i, j: (i, j))],
        out_specs=pl.BlockSpec((tm, tn), lambda i, j: (i, j)),
        out_shape=jax.ShapeDtypeStruct((t, n), F32),
        scratch_shapes=[pltpu.VMEM((tm, dd + ds), BF16)],
        compiler_params=_params(("parallel", "arbitrary"), 56),
        name="mix_out",
    )(o_dil, sa, sb, g_dil, g_sb, w_o, x)


def _t5_bucket(dist):
    n = np.asarray(dist, dtype=np.int64)
    max_exact = N_BUCKETS // 2
    large = max_exact + (np.log(np.maximum(n, 1) / max_exact)
                         / np.log(MAX_DISTANCE / max_exact)
                         * (N_BUCKETS - max_exact)).astype(np.int64)
    large = np.minimum(large, N_BUCKETS - 1)
    return np.where(n < max_exact, n, large).astype(np.int32)


def _dilated_bias_tables(rel_bias):
    n_heads = rel_bias.shape[1]
    period = 3 * WIN
    pad = jnp.full((n_heads, period - WIN - 1), NEG_INF, F32)
    tables = []
    for (w, d) in DILATED_BRANCHES:
        assert w // d == WIN
        bias_m = rel_bias[_t5_bucket(d * np.arange(WIN, -1, -1))].astype(F32).T
        u = jnp.concatenate([bias_m, pad], axis=1)
        shifted = jnp.tile(u, (1, WIN))[:, :WIN * (period - 1)].reshape(n_heads, WIN, period - 1)
        tables.append(shifted[:, :, :2 * WIN])
    return jnp.stack(tables, axis=1)


def _dilated_body(q_ref, k_ref, v_ref, tb_ref, o_ref, ob_ref, lb_ref):
    s = q_ref.shape[0]
    scale = HEAD_DIM ** -0.5

    def load(ref, pos, d):
        return ref[pl.ds(pos, WIN, stride=d), :]

    def attend(bi, d, base, q, kk, vv, tb):
        sc = lax.dot_general(q, kk, (((1,), (1,)), ((), ())), preferred_element_type=F32) + tb
        m = jnp.max(sc, axis=-1, keepdims=True)
        p = jnp.exp(sc - m)
        l = jnp.sum(p, axis=-1, keepdims=True)
        out = jnp.dot(p.astype(BF16), vv, preferred_element_type=F32) * (1.0 / l)
        lse = m + jnp.log(l)
        ob_ref[bi, pl.ds(base, WIN, stride=d), :] = out
        lb_ref[bi, pl.ds(base, WIN, stride=d), :] = jnp.broadcast_to(lse, (WIN, LANES))

    def run(bi, d, start, nb, k_prev, v_prev):
        for b in range(nb):
            base = start + d * WIN * b
            q = (load(q_ref, base, d) * scale).astype(BF16)
            k_cur = load(k_ref, base, d).astype(BF16)
            v_cur = load(v_ref, base, d).astype(BF16)
            if k_prev is None:
                attend(bi, d, base, q, k_cur, v_cur, tb_ref[bi, :, WIN:])
            else:
                attend(bi, d, base, q, jnp.concatenate([k_prev, k_cur], axis=0),
                       jnp.concatenate([v_prev, v_cur], axis=0), tb_ref[bi])
            k_prev, v_prev = k_cur, v_cur

    for bi, (_, d) in enumerate(DILATED_BRANCHES):
        nblk = s // (d * WIN)
        if nblk <= DIL_UNROLL:
            per = DIL_UNROLL // nblk

            def trip(g, carry, bi=bi, d=d, nblk=nblk, per=per):
                for u in range(per):
                    run(bi, d, g * per + u, nblk, None, None)
                return carry

            lax.fori_loop(0, d // per, trip, 0)
        else:
            chunks = nblk // DIL_UNROLL
            span = d * WIN * DIL_UNROLL

            def head(r, carry, bi=bi, d=d):
                run(bi, d, r, DIL_UNROLL, None, None)
                return carry

            def tail(it, carry, bi=bi, d=d, chunks=chunks, span=span):
                start = it // (chunks - 1) + span * (it % (chunks - 1) + 1)
                run(bi, d, start, DIL_UNROLL, load(k_ref, start - d * WIN, d).astype(BF16),
                    load(v_ref, start - d * WIN, d).astype(BF16))
                return carry

            lax.fori_loop(0, d, head, 0)
            lax.fori_loop(0, d * (chunks - 1), tail, 0)

    ch = 256
    def merge(c, carry):
        rows = pl.ds(pl.multiple_of(c * ch, ch), ch)
        ls = [lb_ref[b, rows, :] for b in range(len(DILATED_BRANCHES))]
        mx = functools.reduce(jnp.maximum, ls)
        es = [jnp.exp(x - mx) for x in ls]
        inv = 1.0 / functools.reduce(jnp.add, es)
        acc = es[0] * inv * ob_ref[0, rows, :]
        for b in range(1, len(DILATED_BRANCHES)):
            acc = acc + es[b] * inv * ob_ref[b, rows, :]
        o_ref[rows, :] = acc.astype(o_ref.dtype)
        return carry

    lax.fori_loop(0, s // ch, merge, 0)


def dilated_attention(proj, tables, n_heads):
    b, s, _ = proj.shape
    h = n_heads
    nbr = len(DILATED_BRANCHES)
    return pl.pallas_call(
        _dilated_body,
        grid=(b, h),
        in_specs=[pl.BlockSpec((None, s, HEAD_DIM), lambda bi, hi: (bi, 0, hi)),
                  pl.BlockSpec((None, s, HEAD_DIM), lambda bi, hi: (bi, 0, h + hi)),
                  pl.BlockSpec((None, s, HEAD_DIM), lambda bi, hi: (bi, 0, 2 * h + hi)),
                  pl.BlockSpec((None, nbr, WIN, 2 * WIN), lambda bi, hi: (hi, 0, 0, 0))],
        out_specs=pl.BlockSpec((None, s, HEAD_DIM), lambda bi, hi: (bi, 0, hi)),
        out_shape=jax.ShapeDtypeStruct((b, s, h * HEAD_DIM), BF16),
        scratch_shapes=[pltpu.VMEM((nbr, s, HEAD_DIM), F32),
                        pltpu.VMEM((nbr, s, LANES), F32)],
        compiler_params=_params(("parallel", "parallel"), 48),
        name="dilated_attention",
    )(proj, proj, proj, tables)


SB_BLOCK = 256
LOG2E = 1.4426950408889634


def _softplus2(z2):
    return jnp.maximum(z2, 0.0) + jnp.log2(1.0 + jnp.exp2(-jnp.abs(z2)))


def _sb_body(nq, qa_ref, qb_ref, k_ref, v_ref, oa_ref, ob_ref, q_ref, acc_ref, c_ref):
    blk = SB_BLOCK
    p = pl.program_id(2)
    row = lax.broadcasted_iota(I32, (blk, blk), 0)
    col = lax.broadcasted_iota(I32, (blk, blk), 1)
    tri = (row >= col).astype(BF16)
    causal = col < row
    for slot, src in enumerate((qa_ref, qb_ref)):
        q_ref[slot] = (src[...].astype(F32) * (HEAD_DIM ** -0.5 * LOG2E)).astype(BF16)

    work = [(0, p, True), (1, nq - 1 - p, True)]
    for t in range(nq - 1):
        first = t < p
        work.append((jnp.where(first, 0, 1), jnp.where(first, p - 1 - t, nq - 2 - t), False))
    zs, vs, sums, cs = {}, {}, {}, {}

    def stage_scores(i):
        slot, kb, _ = work[i]
        rows = pl.ds(pl.multiple_of(kb * blk, blk), blk)
        zs[i] = lax.dot_general(q_ref[slot], k_ref[rows, :], (((1,), (1,)), ((), ())),
                                preferred_element_type=F32)
        vs[i] = v_ref[rows, :]

    def stage_sums(i):
        slot, _, diag = work[i]
        sp = _softplus2(zs[i])
        if diag:
            sp = jnp.where(causal, sp, 0.0)
        total = jnp.broadcast_to(jnp.sum(sp, axis=-1, keepdims=True), (blk, LANES))
        if diag:
            c_ref[slot] = total
        else:
            cs[i] = c_ref[slot]
            c_ref[slot] = cs[i] + total
        sums[i] = jnp.dot(sp.astype(BF16), tri, preferred_element_type=F32)

    def stage_out(i):
        slot, _, diag = work[i]
        e = zs.pop(i) - sums.pop(i)
        if diag:
            a = jnp.where(causal, jnp.exp2(e), 0.0)
        else:
            c = cs.pop(i)
            a = jnp.exp2(e - jnp.concatenate([c, c], axis=1))
        pv = jnp.dot(a.astype(BF16), vs.pop(i), preferred_element_type=F32)
        if diag:
            acc_ref[slot] = pv
        else:
            acc_ref[slot] += pv

    n = len(work)
    for step in range(n + 2):
        if step < n:
            stage_scores(step)
        if 0 <= step - 1 < n:
            stage_sums(step - 1)
        if 0 <= step - 2 < n:
            stage_out(step - 2)
    oa_ref[...] = acc_ref[0].astype(oa_ref.dtype)
    ob_ref[...] = acc_ref[1].astype(ob_ref.dtype)


def stick_breaking_attention(proj, n_heads):
    b, s, _ = proj.shape
    h = n_heads
    blk = SB_BLOCK
    nq = s // blk
    assert LANES * 2 == blk and nq % 2 == 0
    q_spec = lambda f: pl.BlockSpec((None, blk, HEAD_DIM), lambda bi, hi, p: (bi, f(p), hi))
    oa, ob = pl.pallas_call(
        functools.partial(_sb_body, nq),
        grid=(b, h, nq // 2),
        in_specs=[q_spec(lambda p: p), q_spec(lambda p: nq - 1 - p),
                  pl.BlockSpec((None, s, HEAD_DIM), lambda bi, hi, p: (bi, 0, h + hi)),
                  pl.BlockSpec((None, s, HEAD_DIM), lambda bi, hi, p: (bi, 0, 2 * h + hi))],
        out_specs=[q_spec(lambda p: p), q_spec(lambda p: nq // 2 - 1 - p)],
        out_shape=[jax.ShapeDtypeStruct((b, s // 2, h * HEAD_DIM), BF16)] * 2,
        scratch_shapes=[pltpu.VMEM((2, blk, HEAD_DIM), BF16),
                        pltpu.VMEM((2, blk, HEAD_DIM), F32),
                        pltpu.VMEM((2, blk, LANES), F32)],
        compiler_params=_params(("parallel", "parallel", "parallel"), 32),
        name="stick_breaking_attention",
    )(proj, proj, proj, proj)
    return oa, ob


MOE_TILE = 1024
DMA_ISSUE_UNROLL = 8
META_I1, META_I2, META_W1, META_W2, META_P1, META_P2 = range(6)


def _router_body(n_experts, x_ref, g_ref, rw_ref, h_ref, meta_ref, cnt_ref, carry_ref):
    tm = x_ref.shape[0]

    @pl.when(pl.program_id(0) == 0)
    def _():
        carry_ref[...] = jnp.zeros_like(carry_ref)

    h = _rms(x_ref[...], g_ref[...])
    h_ref[...] = h
    rw = rw_ref[...]
    h_hi, rw_hi = h.astype(BF16), rw.astype(BF16)
    h_lo = (h - h_hi.astype(F32)).astype(BF16)
    rw_lo = (rw - rw_hi.astype(F32)).astype(BF16)
    logits = (jnp.dot(h_hi, rw_hi, preferred_element_type=F32)
              + (jnp.dot(h_lo, rw_hi, preferred_element_type=F32)
                 + jnp.dot(h_hi, rw_lo, preferred_element_type=F32)))
    lane = lax.broadcasted_iota(I32, logits.shape, 1)
    l1 = jnp.where(lane < n_experts, logits, NEG_INF)
    m1 = jnp.max(l1, axis=-1, keepdims=True)
    i1 = jnp.min(jnp.where(l1 == m1, lane, LANES), axis=-1, keepdims=True)
    l2 = jnp.where(lane == i1, NEG_INF, l1)
    m2 = jnp.max(l2, axis=-1, keepdims=True)
    i2 = jnp.min(jnp.where(l2 == m2, lane, LANES), axis=-1, keepdims=True)
    e2 = jnp.exp(m2 - m1)
    inv = 1.0 / (1.0 + e2)
    w1, w2 = inv, e2 * inv
    sel1, sel2 = lane == i1, lane == i2
    onehot = jnp.where(sel1 | sel2, 1.0, 0.0)
    r = lax.broadcasted_iota(I32, (tm, tm), 0)
    c = lax.broadcasted_iota(I32, (tm, tm), 1)
    before = (c < r).astype(BF16)
    pos = jnp.dot(before, onehot.astype(BF16), preferred_element_type=F32) + carry_ref[...]
    p1 = jnp.sum(jnp.where(sel1, pos, 0.0), axis=-1, keepdims=True)
    p2 = jnp.sum(jnp.where(sel2, pos, 0.0), axis=-1, keepdims=True)
    meta = jnp.zeros_like(logits)
    for idx, val in ((META_I1, i1.astype(F32)), (META_I2, i2.astype(F32)), (META_W1, w1),
                     (META_W2, w2), (META_P1, p1), (META_P2, p2)):
        meta = jnp.where(lane == idx, val, meta)
    meta_ref[...] = meta
    carry_ref[...] += jnp.sum(onehot, axis=0, keepdims=True)
    cnt_ref[...] = carry_ref[...]


def moe_router(x, g, rw, n_experts, tm=512):
    t, d = x.shape
    tm = min(tm, t)
    return pl.pallas_call(
        functools.partial(_router_body, n_experts),
        grid=(t // tm,),
        in_specs=[pl.BlockSpec((tm, d), lambda i: (i, 0)),
                  pl.BlockSpec((1, d), lambda i: (0, 0)),
                  pl.BlockSpec((d, LANES), lambda i: (0, 0))],
        out_specs=[pl.BlockSpec((tm, d), lambda i: (i, 0)),
                   pl.BlockSpec((tm, LANES), lambda i: (i, 0)),
                   pl.BlockSpec((1, LANES), lambda i: (0, 0))],
        out_shape=[jax.ShapeDtypeStruct((t, d), F32),
                   jax.ShapeDtypeStruct((t, LANES), F32),
                   jax.ShapeDtypeStruct((1, LANES), F32)],
        scratch_shapes=[pltpu.VMEM((1, LANES), F32)],
        compiler_params=_params(("arbitrary",), 32),
        name="moe_router",
    )(x, g, rw)


def _row_copy(src_hbm, src_row, dst_ref, dst_row, sem):
    return pltpu.make_async_copy(src_hbm.at[pl.ds(src_row, 1), :],
                                 dst_ref.at[pl.ds(dst_row, 1), :], sem)


def _moe_up_body(te_ref, nv_ref, tok_ref, h_hbm, wg_ref, wu_ref, o_ref, x_ref, h_ref, sem):
    i, j = pl.program_id(0), pl.program_id(1)
    tm = h_ref.shape[0]
    n_valid = nv_ref[0]

    def start_gather(tile, slot):
        base = tile * tm

        def issue(r, carry):
            _row_copy(h_hbm, tok_ref[base + r], x_ref.at[slot], r, sem.at[slot]).start()
            return carry

        lax.fori_loop(0, tm, issue, 0, unroll=DMA_ISSUE_UNROLL)

    @pl.when((i == 0) & (j == 0))
    def _():
        start_gather(0, 0)

    @pl.when(i < n_valid)
    def _():
        @pl.when(j == 0)
        def _():
            slot = i % 2
            pltpu.make_async_copy(h_hbm.at[pl.ds(0, tm), :], x_ref.at[slot], sem.at[slot]).wait()
            h_ref[...] = x_ref[slot].astype(BF16)

            @pl.when(i + 1 < n_valid)
            def _():
                start_gather(i + 1, 1 - slot)

        h = h_ref[...]
        a = jnp.dot(h, wg_ref[...], preferred_element_type=F32)
        b = jnp.dot(h, wu_ref[...], preferred_element_type=F32)
        o_ref[...] = _silu_mul(a, b).astype(o_ref.dtype)

    @pl.when(i >= nv_ref[0])
    def _():
        o_ref[...] = jnp.zeros_like(o_ref)


def moe_up(h, tok, tile_expert, n_valid, wg, wu, tn=1024):
    d = h.shape[1]
    rows = tok.shape[0]
    f = wg.shape[2]
    tm, tn = MOE_TILE, min(tn, f)
    return pl.pallas_call(
        _moe_up_body,
        grid_spec=pltpu.PrefetchScalarGridSpec(
            num_scalar_prefetch=3,
            grid=(rows // tm, f // tn),
            in_specs=[pl.BlockSpec(memory_space=pl.ANY),
                      pl.BlockSpec((None, d, tn), lambda i, j, te, nv, tok: (te[i], 0, j)),
                      pl.BlockSpec((None, d, tn), lambda i, j, te, nv, tok: (te[i], 0, j))],
            out_specs=pl.BlockSpec((tm, tn), lambda i, j, te, nv, tok: (i, j)),
            scratch_shapes=[pltpu.VMEM((2, tm, d), F32),
                            pltpu.VMEM((tm, d), BF16),
                            pltpu.SemaphoreType.DMA((2,))]),
        out_shape=jax.ShapeDtypeStruct((rows, f), BF16),
        compiler_params=_params(("arbitrary", "arbitrary"), 56),
        name="moe_up",
    )(tile_expert, n_valid, tok, h, wg, wu)


def _moe_down_body(te_ref, nv_ref, a_ref, w_ref, o_ref):
    i = pl.program_id(0)

    @pl.when(i < nv_ref[0])
    def _():
        o_ref[...] = jnp.dot(a_ref[...], w_ref[...], preferred_element_type=F32)

    @pl.when(i >= nv_ref[0])
    def _():
        o_ref[...] = jnp.zeros_like(o_ref)


def moe_down(act, tile_expert, n_valid, wd, tn=256):
    rows, f = act.shape
    d = wd.shape[2]
    tm, tn = MOE_TILE, min(tn, d)
    return pl.pallas_call(
        _moe_down_body,
        grid_spec=pltpu.PrefetchScalarGridSpec(
            num_scalar_prefetch=2,
            grid=(rows // tm, d // tn),
            in_specs=[pl.BlockSpec((tm, f), lambda i, j, te, nv: (i, 0)),
                      pl.BlockSpec((None, f, tn), lambda i, j, te, nv: (te[i], 0, j))],
            out_specs=pl.BlockSpec((tm, tn), lambda i, j, te, nv: (i, j))),
        out_shape=jax.ShapeDtypeStruct((rows, d), F32),
        compiler_params=_params(("parallel", "arbitrary"), 56),
        name="moe_down",
    )(tile_expert, n_valid, act, wd)


def _combine_body(d1_ref, d2_ref, x_ref, meta_ref, g_ref, y_hbm, o_ref, buf_ref, sem):
    tc = x_ref.shape[0]
    base = pl.program_id(0) * tc

    def issue(r, carry):
        _row_copy(y_hbm, d1_ref[base + r], buf_ref.at[0], r, sem).start()
        _row_copy(y_hbm, d2_ref[base + r], buf_ref.at[1], r, sem).start()
        return carry

    lax.fori_loop(0, tc, issue, 0, unroll=DMA_ISSUE_UNROLL)
    for k in range(TOP_K):
        pltpu.make_async_copy(y_hbm.at[pl.ds(0, tc), :], buf_ref.at[k], sem).wait()
    meta = meta_ref[...]
    w1 = meta[:, META_W1:META_W1 + 1]
    w2 = meta[:, META_W2:META_W2 + 1]
    y = x_ref[...] + (w1 * buf_ref[0] + w2 * buf_ref[1])
    o_ref[...] = _rms(y, g_ref[...])


def moe_combine_norm(x, meta, d1, d2, y, g, tc=256):
    t, d = x.shape
    tc = min(tc, t)
    return pl.pallas_call(
        _combine_body,
        grid_spec=pltpu.PrefetchScalarGridSpec(
            num_scalar_prefetch=2,
            grid=(t // tc,),
            in_specs=[pl.BlockSpec((tc, d), lambda i, a, b: (i, 0)),
                      pl.BlockSpec((tc, LANES), lambda i, a, b: (i, 0)),
                      pl.BlockSpec((1, d), lambda i, a, b: (0, 0)),
                      pl.BlockSpec(memory_space=pl.ANY)],
            out_specs=pl.BlockSpec((tc, d), lambda i, a, b: (i, 0)),
            scratch_shapes=[pltpu.VMEM((2, tc, d), F32),
                            pltpu.SemaphoreType.DMA(())]),
        out_shape=jax.ShapeDtypeStruct((t, d), F32),
        compiler_params=_params(("arbitrary",), 32),
        name="moe_combine_norm",
    )(d1, d2, x, meta, g, y)


def moe_layer_and_final_norm(x, g, router_w, w_gate, w_up, w_down, final_g):
    t, d = x.shape
    n_experts = router_w.shape[1]
    tm = MOE_TILE
    rw = jnp.zeros((d, LANES), F32).at[:, :n_experts].set(router_w)
    h, meta, cnt = moe_router(x, g, rw, n_experts)

    rows = t * TOP_K + n_experts * tm
    n_tiles = rows // tm
    counts = cnt[0, :n_experts].astype(I32)
    tiles_per = (counts + tm - 1) // tm
    tile_end = jnp.cumsum(tiles_per)
    offs = (tile_end - tiles_per) * tm
    n_valid = tile_end[-1:]
    tile_expert = jnp.minimum(
        jnp.searchsorted(tile_end, jnp.arange(n_tiles, dtype=I32), side="right"),
        n_experts - 1).astype(I32)
    i1 = meta[:, META_I1].astype(I32)
    i2 = meta[:, META_I2].astype(I32)
    d1 = offs[i1] + meta[:, META_P1].astype(I32)
    d2 = offs[i2] + meta[:, META_P2].astype(I32)
    tid = jnp.arange(t, dtype=I32)
    tok = jnp.zeros((rows,), I32).at[jnp.concatenate([d1, d2])].set(jnp.concatenate([tid, tid]))

    act = moe_up(h, tok, tile_expert, n_valid, w_gate, w_up)
    y = moe_down(act, tile_expert, n_valid, w_down)
    return moe_combine_norm(x, meta, d1, d2, y, final_g)


def kernel(x, rel_bias, attn_norm_g, w_in, mix_norm_dil_g, mix_norm_sb_g, w_o, ffn_norm_g,
           dense_w_gate, dense_w_up, dense_w_down, router_w, moe_w_gate, moe_w_up,
           moe_w_down, final_norm_g):
    b, s, d = x.shape
    depth = w_in.shape[0]
    assert depth == 2, "layer 0 dense SwiGLU, layer 1 routed experts followed by the final norm"
    d_dil = mix_norm_dil_g.shape[1]
    d_sb = mix_norm_sb_g.shape[1]
    h_dil, h_sb = d_dil // HEAD_DIM, d_sb // HEAD_DIM
    tables = _dilated_bias_tables(rel_bias)
    row = lambda v: v.reshape(1, -1)

    xt = x.reshape(b * s, d)
    for layer in range(depth):
        w_in_l = w_in[layer].astype(BF16)
        g_attn = row(attn_norm_g[layer])
        p_dil, p_sb = norm_matmul_split(xt, g_attn, w_in_l, 3 * d_dil, F32, BF16)
        o_dil = dilated_attention(p_dil.reshape(b, s, 3 * d_dil), tables, h_dil)
        o_sb = stick_breaking_attention(p_sb.reshape(b, s, 3 * d_sb), h_sb)
        xt = mix_out(o_dil.reshape(b * s, d_dil),
                     tuple(o.reshape(b * s // 2, d_sb) for o in o_sb),
                     row(mix_norm_dil_g[layer]), row(mix_norm_sb_g[layer]),
                     w_o[layer].astype(BF16), xt, s)
        g_ffn = row(ffn_norm_g[layer])
        j = layer // 2
        if layer % 2 == 0:
            act = ffn_up(xt, g_ffn, dense_w_gate[j].astype(BF16), dense_w_up[j].astype(BF16))
            xt = matmul_residual(act, dense_w_down[j].astype(BF16), xt)
        else:
            xt = moe_layer_and_final_norm(
                xt, g_ffn, router_w[j], moe_w_gate[j].astype(BF16), moe_w_up[j].astype(BF16),
                moe_w_down[j].astype(BF16), row(final_norm_g))
    return xt.reshape(b, s, d)
```

```python
import functools

import numpy as np
import jax
import jax.numpy as jnp
from jax import lax
from jax.experimental import pallas as pl
from jax.experimental.pallas import tpu as pltpu

F32 = jnp.float32
BF16 = jnp.bfloat16
I32 = jnp.int32

EPS = 1e-6
HEAD_DIM = 128
DILATED_BRANCHES = ((128, 1), (512, 4), (2048, 16))
N_BUCKETS = 32
MAX_DISTANCE = 2048
TOP_K = 2
LANES = 128
WIN = 128
NEG_INF = float("-inf")
DIL_UNROLL = 8
MIB = 1024 * 1024


def _params(semantics, vmem_mib):
    return pltpu.CompilerParams(dimension_semantics=semantics,
                                vmem_limit_bytes=vmem_mib * MIB)


def _rms(x, g):
    return x * lax.rsqrt(jnp.mean(x * x, axis=-1, keepdims=True) + EPS) * g


def _norm_matmul_body(n_first, x_ref, g_ref, w_ref, oa_ref, ob_ref, h_ref):
    j = pl.program_id(1)

    @pl.when(j == 0)
    def _():
        h_ref[...] = _rms(x_ref[...], g_ref[...]).astype(BF16)

    y = jnp.dot(h_ref[...], w_ref[...], preferred_element_type=F32)

    @pl.when(j < n_first)
    def _():
        oa_ref[...] = y.astype(oa_ref.dtype)

    @pl.when(j >= n_first)
    def _():
        ob_ref[...] = y.astype(ob_ref.dtype)


def norm_matmul_split(x, g, w, n_a, dtype_a, dtype_b, tm=1024, tn=1024):
    t, d = x.shape
    n = w.shape[1]
    tm, tn = min(tm, t), min(tn, n_a, n - n_a)
    assert n_a % tn == 0 and (n - n_a) % tn == 0
    ja = n_a // tn
    return pl.pallas_call(
        functools.partial(_norm_matmul_body, ja),
        grid=(t // tm, n // tn),
        in_specs=[pl.BlockSpec((tm, d), lambda i, j: (i, 0)),
                  pl.BlockSpec((1, d), lambda i, j: (0, 0)),
                  pl.BlockSpec((d, tn), lambda i, j: (0, j))],
        out_specs=[pl.BlockSpec((tm, tn), lambda i, j: (i, jnp.minimum(j, ja - 1))),
                   pl.BlockSpec((tm, tn), lambda i, j: (i, jnp.maximum(j - ja, 0)))],
        out_shape=[jax.ShapeDtypeStruct((t, n_a), dtype_a),
                   jax.ShapeDtypeStruct((t, n - n_a), dtype_b)],
        scratch_shapes=[pltpu.VMEM((tm, d), BF16)],
        compiler_params=_params(("parallel", "arbitrary"), 56),
        name="norm_matmul",
    )(x, g, w)


def _silu_mul(a, b):
    return a * (1.0 / (1.0 + jnp.exp(-a))) * b


def _ffn_up_body(x_ref, g_ref, wg_ref, wu_ref, o_ref, h_ref):
    @pl.when(pl.program_id(1) == 0)
    def _():
        h_ref[...] = _rms(x_ref[...], g_ref[...]).astype(BF16)

    h = h_ref[...]
    a = jnp.dot(h, wg_ref[...], preferred_element_type=F32)
    b = jnp.dot(h, wu_ref[...], preferred_element_type=F32)
    o_ref[...] = _silu_mul(a, b).astype(o_ref.dtype)


def ffn_up(x, g, wg, wu, tm=1024, tn=512):
    t, d = x.shape
    f = wg.shape[1]
    tm, tn = min(tm, t), min(tn, f)
    return pl.pallas_call(
        _ffn_up_body,
        grid=(t // tm, pl.cdiv(f, tn)),
        in_specs=[pl.BlockSpec((tm, d), lambda i, j: (i, 0)),
                  pl.BlockSpec((1, d), lambda i, j: (0, 0)),
                  pl.BlockSpec((d, tn), lambda i, j: (0, j)),
                  pl.BlockSpec((d, tn), lambda i, j: (0, j))],
        out_specs=pl.BlockSpec((tm, tn), lambda i, j: (i, j)),
        out_shape=jax.ShapeDtypeStruct((t, f), BF16),
        scratch_shapes=[pltpu.VMEM((tm, d), BF16)],
        compiler_params=_params(("parallel", "arbitrary"), 56),
        name="ffn_up",
    )(x, g, wg, wu)


def _matmul_res_body(a_ref, w_ref, r_ref, o_ref):
    o_ref[...] = r_ref[...] + jnp.dot(a_ref[...], w_ref[...], preferred_element_type=F32)


def matmul_residual(a, w, res, tm=512, tn=1024):
    t, f = a.shape
    n = w.shape[1]
    tm, tn = min(tm, t), min(tn, n)
    return pl.pallas_call(
        _matmul_res_body,
        grid=(n // tn, t // tm),
        in_specs=[pl.BlockSpec((tm, f), lambda j, i: (i, 0)),
                  pl.BlockSpec((f, tn), lambda j, i: (0, j)),
                  pl.BlockSpec((tm, tn), lambda j, i: (i, j))],
        out_specs=pl.BlockSpec((tm, tn), lambda j, i: (i, j)),
        out_shape=jax.ShapeDtypeStruct((t, n), F32),
        compiler_params=_params(("parallel", "parallel"), 52),
        name="matmul_residual",
    )(a, w, res)


def _mix_out_body(tiles_per_seq, od_ref, sa_ref, sb_ref, gd_ref, gs_ref, w_ref, x_ref, o_ref, h_ref):
    dd = od_ref.shape[1]
    half = tiles_per_seq // 2
    si = pl.program_id(0) % tiles_per_seq

    @pl.when(pl.program_id(1) == 0)
    def _():
        h_ref[:, :dd] = _rms(od_ref[...].astype(F32), gd_ref[...]).astype(BF16)

        @pl.when(si < half)
        def _():
            h_ref[:, dd:] = _rms(sa_ref[...].astype(F32), gs_ref[...]).astype(BF16)

        @pl.when(si >= half)
        def _():
            h_ref[:, dd:] = _rms(sb_ref[...].astype(F32), gs_ref[...]).astype(BF16)

    o_ref[...] = x_ref[...] + jnp.dot(h_ref[...], w_ref[...], preferred_element_type=F32)


def mix_out(o_dil, o_sb_halves, g_dil, g_sb, w_o, x, seq, tm=512, tn=2048):
    t, dd = o_dil.shape
    sa, sb = o_sb_halves
    ds = sa.shape[1]
    n = w_o.shape[1]
    tm, tn = min(tm, seq // 2), min(tn, n)
    tps = seq // tm
    half = tps // 2
    return pl.pallas_call(
        functools.partial(_mix_out_body, tps),
        grid=(t // tm, n // tn),
        in_specs=[pl.BlockSpec((tm, dd), lambda i, j: (i, 0)),
                  pl.BlockSpec((tm, ds), lambda i, j: ((i // tps) * half + jnp.minimum(i % tps, half - 1), 0)),
                  pl.BlockSpec((tm, ds), lambda i, j: ((i // tps) * half + jnp.maximum(i % tps - half, 0), 0)),
                  pl.BlockSpec((1, dd), lambda i, j: (0, 0)),
                  pl.BlockSpec((1, ds), lambda i, j: (0, 0)),
                  pl.BlockSpec((dd + ds, tn), lambda i, j: (0, j)),
                  pl.BlockSpec((tm, tn), lambda i, j: (i, j))],
        out_specs=pl.BlockSpec((tm, tn), lambda i, j: (i, j)),
        out_shape=jax.ShapeDtypeStruct((t, n), F32),
        scratch_shapes=[pltpu.VMEM((tm, dd + ds), BF16)],
        compiler_params=_params(("parallel", "arbitrary"), 56),
        name="mix_out",
    )(o_dil, sa, sb, g_dil, g_sb, w_o, x)


def _t5_bucket(dist):
    n = np.asarray(dist, dtype=np.int64)
    max_exact = N_BUCKETS // 2
    large = max_exact + (np.log(np.maximum(n, 1) / max_exact)
                         / np.log(MAX_DISTANCE / max_exact)
                         * (N_BUCKETS - max_exact)).astype(np.int64)
    large = np.minimum(large, N_BUCKETS - 1)
    return np.where(n < max_exact, n, large).astype(np.int32)


def _dilated_bias_tables(rel_bias):
    n_heads = rel_bias.shape[1]
    period = 3 * WIN
    pad = jnp.full((n_heads, period - WIN - 1), NEG_INF, F32)
    tables = []
    for (w, d) in DILATED_BRANCHES:
        assert w // d == WIN
        bias_m = rel_bias[_t5_bucket(d * np.arange(WIN, -1, -1))].astype(F32).T
        u = jnp.concatenate([bias_m, pad], axis=1)
        shifted = jnp.tile(u, (1, WIN))[:, :WIN * (period - 1)].reshape(n_heads, WIN, period - 1)
        tables.append(shifted[:, :, :2 * WIN])
    return jnp.stack(tables, axis=1)


def _dilated_body(q_ref, k_ref, v_ref, tb_ref, o_ref, ob_ref, lb_ref):
    s = q_ref.shape[0]
    scale = HEAD_DIM ** -0.5

    def load(ref, pos, d):
        return ref[pl.ds(pos, WIN, stride=d), :]

    def attend(bi, d, base, q, kk, vv, tb):
        sc = lax.dot_general(q, kk, (((1,), (1,)), ((), ())), preferred_element_type=F32) + tb
        m = jnp.max(sc, axis=-1, keepdims=True)
        p = jnp.exp(sc - m)
        l = jnp.sum(p, axis=-1, keepdims=True)
        out = jnp.dot(p.astype(BF16), vv, preferred_element_type=F32) * (1.0 / l)
        lse = m + jnp.log(l)
        ob_ref[bi, pl.ds(base, WIN, stride=d), :] = out
        lb_ref[bi, pl.ds(base, WIN, stride=d), :] = jnp.broadcast_to(lse, (WIN, LANES))

    def run(bi, d, start, nb, k_prev, v_prev):
        for b in range(nb):
            base = start + d * WIN * b
            q = (load(q_ref, base, d) * scale).astype(BF16)
            k_cur = load(k_ref, base, d).astype(BF16)
            v_cur = load(v_ref, base, d).astype(BF16)
            if k_prev is None:
                attend(bi, d, base, q, k_cur, v_cur, tb_ref[bi, :, WIN:])
            else:
                attend(bi, d, base, q, jnp.concatenate([k_prev, k_cur], axis=0),
                       jnp.concatenate([v_prev, v_cur], axis=0), tb_ref[bi])
            k_prev, v_prev = k_cur, v_cur

    for bi, (_, d) in enumerate(DILATED_BRANCHES):
        nblk = s // (d * WIN)
        if nblk <= DIL_UNROLL:
            per = DIL_UNROLL // nblk

            def trip(g, carry, bi=bi, d=d, nblk=nblk, per=per):
                for u in range(per):
                    run(bi, d, g * per + u, nblk, None, None)
                return carry

            lax.fori_loop(0, d // per, trip, 0)
        else:
            chunks = nblk // DIL_UNROLL
            span = d * WIN * DIL_UNROLL

            def head(r, carry, bi=bi, d=d):
                run(bi, d, r, DIL_UNROLL, None, None)
                return carry

            def tail(it, carry, bi=bi, d=d, chunks=chunks, span=span):
                start = it // (chunks - 1) + span * (it % (chunks - 1) + 1)
                run(bi, d, start, DIL_UNROLL, load(k_ref, start - d * WIN, d).astype(BF16),
                    load(v_ref, start - d * WIN, d).astype(BF16))
                return carry

            lax.fori_loop(0, d, head, 0)
            lax.fori_loop(0, d * (chunks - 1), tail, 0)

    ch = 256
    def merge(c, carry):
        rows = pl.ds(pl.multiple_of(c * ch, ch), ch)
        ls = [lb_ref[b, rows, :] for b in range(len(DILATED_BRANCHES))]
        mx = functools.reduce(jnp.maximum, ls)
        es = [jnp.exp(x - mx) for x in ls]
        inv = 1.0 / functools.reduce(jnp.add, es)
        acc = es[0] * inv * ob_ref[0, rows, :]
        for b in range(1, len(DILATED_BRANCHES)):
            acc = acc + es[b] * inv * ob_ref[b, rows, :]
        o_ref[rows, :] = acc.astype(o_ref.dtype)
        return carry

    lax.fori_loop(0, s // ch, merge, 0)


def dilated_attention(proj, tables, n_heads):
    b, s, _ = proj.shape
    h = n_heads
    nbr = len(DILATED_BRANCHES)
    return pl.pallas_call(
        _dilated_body,
        grid=(b, h),
        in_specs=[pl.BlockSpec((None, s, HEAD_DIM), lambda bi, hi: (bi, 0, hi)),
                  pl.BlockSpec((None, s, HEAD_DIM), lambda bi, hi: (bi, 0, h + hi)),
                  pl.BlockSpec((None, s, HEAD_DIM), lambda bi, hi: (bi, 0, 2 * h + hi)),
                  pl.BlockSpec((None, nbr, WIN, 2 * WIN), lambda bi, hi: (hi, 0, 0, 0))],
        out_specs=pl.BlockSpec((None, s, HEAD_DIM), lambda bi, hi: (bi, 0, hi)),
        out_shape=jax.ShapeDtypeStruct((b, s, h * HEAD_DIM), BF16),
        scratch_shapes=[pltpu.VMEM((nbr, s, HEAD_DIM), F32),
                        pltpu.VMEM((nbr, s, LANES), F32)],
        compiler_params=_params(("parallel", "parallel"), 48),
        name="dilated_attention",
    )(proj, proj, proj, tables)


SB_BLOCK = 256
LOG2E = 1.4426950408889634


def _softplus2(z2):
    return jnp.maximum(z2, 0.0) + jnp.log2(1.0 + jnp.exp2(-jnp.abs(z2)))


def _sb_body(nq, qa_ref, qb_ref, k_ref, v_ref, oa_ref, ob_ref, q_ref, acc_ref, c_ref):
    blk = SB_BLOCK
    p = pl.program_id(2)
    row = lax.broadcasted_iota(I32, (blk, blk), 0)
    col = lax.broadcasted_iota(I32, (blk, blk), 1)
    tri = (row >= col).astype(BF16)
    causal = col < row
    for slot, src in enumerate((qa_ref, qb_ref)):
        q_ref[slot] = (src[...].astype(F32) * (HEAD_DIM ** -0.5 * LOG2E)).astype(BF16)

    work = [(0, p, True), (1, nq - 1 - p, True)]
    for t in range(nq - 1):
        first = t < p
        work.append((jnp.where(first, 0, 1), jnp.where(first, p - 1 - t, nq - 2 - t), False))
    zs, vs, sums, cs = {}, {}, {}, {}

    def stage_scores(i):
        slot, kb, _ = work[i]
        rows = pl.ds(pl.multiple_of(kb * blk, blk), blk)
        zs[i] = lax.dot_general(q_ref[slot], k_ref[rows, :], (((1,), (1,)), ((), ())),
                                preferred_element_type=F32)
        vs[i] = v_ref[rows, :]

    def stage_sums(i):
        slot, _, diag = work[i]
        sp = _softplus2(zs[i])
        if diag:
            sp = jnp.where(causal, sp, 0.0)
        total = jnp.broadcast_to(jnp.sum(sp, axis=-1, keepdims=True), (blk, LANES))
        if diag:
            c_ref[slot] = total
        else:
            cs[i] = c_ref[slot]
            c_ref[slot] = cs[i] + total
        sums[i] = jnp.dot(sp.astype(BF16), tri, preferred_element_type=F32)

    def stage_out(i):
        slot, _, diag = work[i]
        e = zs.pop(i) - sums.pop(i)
        if diag:
            a = jnp.where(causal, jnp.exp2(e), 0.0)
        else:
            c = cs.pop(i)
            a = jnp.exp2(e - jnp.concatenate([c, c], axis=1))
        pv = jnp.dot(a.astype(BF16), vs.pop(i), preferred_element_type=F32)
        if diag:
            acc_ref[slot] = pv
        else:
            acc_ref[slot] += pv

    n = len(work)
    for step in range(n + 2):
        if step < n:
            stage_scores(step)
        if 0 <= step - 1 < n:
            stage_sums(step - 1)
        if 0 <= step - 2 < n:
            stage_out(step - 2)
    oa_ref[...] = acc_ref[0].astype(oa_ref.dtype)
    ob_ref[...] = acc_ref[1].astype(ob_ref.dtype)


def stick_breaking_attention(proj, n_heads):
    b, s, _ = proj.shape
    h = n_heads
    blk = SB_BLOCK
    nq = s // blk
    assert LANES * 2 == blk and nq % 2 == 0
    q_spec = lambda f: pl.BlockSpec((None, blk, HEAD_DIM), lambda bi, hi, p: (bi, f(p), hi))
    oa, ob = pl.pallas_call(
        functools.partial(_sb_body, nq),
        grid=(b, h, nq // 2),
        in_specs=[q_spec(lambda p: p), q_spec(lambda p: nq - 1 - p),
                  pl.BlockSpec((None, s, HEAD_DIM), lambda bi, hi, p: (bi, 0, h + hi)),
                  pl.BlockSpec((None, s, HEAD_DIM), lambda bi, hi, p: (bi, 0, 2 * h + hi))],
        out_specs=[q_spec(lambda p: p), q_spec(lambda p: nq // 2 - 1 - p)],
        out_shape=[jax.ShapeDtypeStruct((b, s // 2, h * HEAD_DIM), BF16)] * 2,
        scratch_shapes=[pltpu.VMEM((2, blk, HEAD_DIM), BF16),
                        pltpu.VMEM((2, blk, HEAD_DIM), F32),
                        pltpu.VMEM((2, blk, LANES), F32)],
        compiler_params=_params(("parallel", "parallel", "parallel"), 32),
        name="stick_breaking_attention",
    )(proj, proj, proj, proj)
    return oa, ob


MOE_TILE = 512
DMA_ISSUE_UNROLL = 8
META_I1, META_I2, META_W1, META_W2, META_P1, META_P2 = range(6)


def _router_body(n_experts, x_ref, g_ref, rw_ref, h_ref, meta_ref, cnt_ref, carry_ref):
    tm = x_ref.shape[0]

    @pl.when(pl.program_id(0) == 0)
    def _():
        carry_ref[...] = jnp.zeros_like(carry_ref)

    h = _rms(x_ref[...], g_ref[...])
    h_ref[...] = h
    rw = rw_ref[...]
    h_hi, rw_hi = h.astype(BF16), rw.astype(BF16)
    h_lo = (h - h_hi.astype(F32)).astype(BF16)
    rw_lo = (rw - rw_hi.astype(F32)).astype(BF16)
    logits = (jnp.dot(h_hi, rw_hi, preferred_element_type=F32)
              + (jnp.dot(h_lo, rw_hi, preferred_element_type=F32)
                 + jnp.dot(h_hi, rw_lo, preferred_element_type=F32)))
    lane = lax.broadcasted_iota(I32, logits.shape, 1)
    l1 = jnp.where(lane < n_experts, logits, NEG_INF)
    m1 = jnp.max(l1, axis=-1, keepdims=True)
    i1 = jnp.min(jnp.where(l1 == m1, lane, LANES), axis=-1, keepdims=True)
    l2 = jnp.where(lane == i1, NEG_INF, l1)
    m2 = jnp.max(l2, axis=-1, keepdims=True)
    i2 = jnp.min(jnp.where(l2 == m2, lane, LANES), axis=-1, keepdims=True)
    e2 = jnp.exp(m2 - m1)
    inv = 1.0 / (1.0 + e2)
    w1, w2 = inv, e2 * inv
    sel1, sel2 = lane == i1, lane == i2
    onehot = jnp.where(sel1 | sel2, 1.0, 0.0)
    r = lax.broadcasted_iota(I32, (tm, tm), 0)
    c = lax.broadcasted_iota(I32, (tm, tm), 1)
    before = (c < r).astype(BF16)
    pos = jnp.dot(before, onehot.astype(BF16), preferred_element_type=F32) + carry_ref[...]
    p1 = jnp.sum(jnp.where(sel1, pos, 0.0), axis=-1, keepdims=True)
    p2 = jnp.sum(jnp.where(sel2, pos, 0.0), axis=-1, keepdims=True)
    meta = jnp.zeros_like(logits)
    for idx, val in ((META_I1, i1.astype(F32)), (META_I2, i2.astype(F32)), (META_W1, w1),
                     (META_W2, w2), (META_P1, p1), (META_P2, p2)):
        meta = jnp.where(lane == idx, val, meta)
    meta_ref[...] = meta
    carry_ref[...] += jnp.sum(onehot, axis=0, keepdims=True)
    cnt_ref[...] = carry_ref[...]


def moe_router(x, g, rw, n_experts, tm=512):
    t, d = x.shape
    tm = min(tm, t)
    return pl.pallas_call(
        functools.partial(_router_body, n_experts),
        grid=(t // tm,),
        in_specs=[pl.BlockSpec((tm, d), lambda i: (i, 0)),
                  pl.BlockSpec((1, d), lambda i: (0, 0)),
                  pl.BlockSpec((d, LANES), lambda i: (0, 0))],
        out_specs=[pl.BlockSpec((tm, d), lambda i: (i, 0)),
                   pl.BlockSpec((tm, LANES), lambda i: (i, 0)),
                   pl.BlockSpec((1, LANES), lambda i: (0, 0))],
        out_shape=[jax.ShapeDtypeStruct((t, d), F32),
                   jax.ShapeDtypeStruct((t, LANES), F32),
                   jax.ShapeDtypeStruct((1, LANES), F32)],
        scratch_shapes=[pltpu.VMEM((1, LANES), F32)],
        compiler_params=_params(("arbitrary",), 32),
        name="moe_router",
    )(x, g, rw)


def _row_copy(src_hbm, src_row, dst_ref, dst_row, sem):
    return pltpu.make_async_copy(src_hbm.at[pl.ds(src_row, 1), :],
                                 dst_ref.at[pl.ds(dst_row, 1), :], sem)


def _moe_up_body(te_ref, nv_ref, tok_ref, h_hbm, wg_ref, wu_ref, o_ref, x_ref, h_ref, sem):
    i, j = pl.program_id(0), pl.program_id(1)
    tm = h_ref.shape[0]
    n_valid = nv_ref[0]

    def start_gather(tile, slot):
        base = tile * tm

        def issue(r, carry):
            _row_copy(h_hbm, tok_ref[base + r], x_ref.at[slot], r, sem.at[slot]).start()
            return carry

        lax.fori_loop(0, tm, issue, 0, unroll=DMA_ISSUE_UNROLL)

    @pl.when((i == 0) & (j == 0))
    def _():
        start_gather(0, 0)

    @pl.when(i < n_valid)
    def _():
        @pl.when(j == 0)
        def _():
            slot = i % 2
            pltpu.make_async_copy(h_hbm.at[pl.ds(0, tm), :], x_ref.at[slot], sem.at[slot]).wait()
            h_ref[...] = x_ref[slot].astype(BF16)

            @pl.when(i + 1 < n_valid)
            def _():
                start_gather(i + 1, 1 - slot)

        h = h_ref[...]
        a = jnp.dot(h, wg_ref[...], preferred_element_type=F32)
        b = jnp.dot(h, wu_ref[...], preferred_element_type=F32)
        o_ref[...] = _silu_mul(a, b).astype(o_ref.dtype)

    @pl.when(i >= nv_ref[0])
    def _():
        o_ref[...] = jnp.zeros_like(o_ref)


def moe_up(h, tok, tile_expert, n_valid, wg, wu, tn=1792):
    d = h.shape[1]
    rows = tok.shape[0]
    f = wg.shape[2]
    tm, tn = MOE_TILE, min(tn, f)
    return pl.pallas_call(
        _moe_up_body,
        grid_spec=pltpu.PrefetchScalarGridSpec(
            num_scalar_prefetch=3,
            grid=(rows // tm, f // tn),
            in_specs=[pl.BlockSpec(memory_space=pl.ANY),
                      pl.BlockSpec((None, d, tn), lambda i, j, te, nv, tok: (te[i], 0, j)),
                      pl.BlockSpec((None, d, tn), lambda i, j, te, nv, tok: (te[i], 0, j))],
            out_specs=pl.BlockSpec((tm, tn), lambda i, j, te, nv, tok: (i, j)),
            scratch_shapes=[pltpu.VMEM((2, tm, d), F32),
                            pltpu.VMEM((tm, d), BF16),
                            pltpu.SemaphoreType.DMA((2,))]),
        out_shape=jax.ShapeDtypeStruct((rows, f), BF16),
        compiler_params=_params(("arbitrary", "arbitrary"), 56),
        name="moe_up",
    )(tile_expert, n_valid, tok, h, wg, wu)


def _moe_down_body(te_ref, nv_ref, a_ref, w_ref, o_ref):
    i = pl.program_id(0)

    @pl.when(i < nv_ref[0])
    def _():
        o_ref[...] = jnp.dot(a_ref[...], w_ref[...], preferred_element_type=F32)

    @pl.when(i >= nv_ref[0])
    def _():
        o_ref[...] = jnp.zeros_like(o_ref)


def moe_down(act, tile_expert, n_valid, wd, tn=512):
    rows, f = act.shape
    d = wd.shape[2]
    tm, tn = MOE_TILE, min(tn, d)
    return pl.pallas_call(
        _moe_down_body,
        grid_spec=pltpu.PrefetchScalarGridSpec(
            num_scalar_prefetch=2,
            grid=(rows // tm, d // tn),
            in_specs=[pl.BlockSpec((tm, f), lambda i, j, te, nv: (i, 0)),
                      pl.BlockSpec((None, f, tn), lambda i, j, te, nv: (te[i], 0, j))],
            out_specs=pl.BlockSpec((tm, tn), lambda i, j, te, nv: (i, j))),
        out_shape=jax.ShapeDtypeStruct((rows, d), F32),
        compiler_params=_params(("parallel", "arbitrary"), 56),
        name="moe_down",
    )(tile_expert, n_valid, act, wd)


def _combine_body(d1_ref, d2_ref, x_ref, meta_ref, g_ref, y_hbm, o_ref, buf_ref, sem):
    tc = x_ref.shape[0]
    base = pl.program_id(0) * tc

    def issue(r, carry):
        _row_copy(y_hbm, d1_ref[base + r], buf_ref.at[0], r, sem).start()
        _row_copy(y_hbm, d2_ref[base + r], buf_ref.at[1], r, sem).start()
        return carry

    lax.fori_loop(0, tc, issue, 0, unroll=DMA_ISSUE_UNROLL)
    for k in range(TOP_K):
        pltpu.make_async_copy(y_hbm.at[pl.ds(0, tc), :], buf_ref.at[k], sem).wait()
    meta = meta_ref[...]
    w1 = meta[:, META_W1:META_W1 + 1]
    w2 = meta[:, META_W2:META_W2 + 1]
    y = x_ref[...] + (w1 * buf_ref[0] + w2 * buf_ref[1])
    o_ref[...] = _rms(y, g_ref[...])


def moe_combine_norm(x, meta, d1, d2, y, g, tc=256):
    t, d = x.shape
    tc = min(tc, t)
    return pl.pallas_call(
        _combine_body,
        grid_spec=pltpu.PrefetchScalarGridSpec(
            num_scalar_prefetch=2,
            grid=(t // tc,),
            in_specs=[pl.BlockSpec((tc, d), lambda i, a, b: (i, 0)),
                      pl.BlockSpec((tc, LANES), lambda i, a, b: (i, 0)),
                      pl.BlockSpec((1, d), lambda i, a, b: (0, 0)),
                      pl.BlockSpec(memory_space=pl.ANY)],
            out_specs=pl.BlockSpec((tc, d), lambda i, a, b: (i, 0)),
            scratch_shapes=[pltpu.VMEM((2, tc, d), F32),
                            pltpu.SemaphoreType.DMA(())]),
        out_shape=jax.ShapeDtypeStruct((t, d), F32),
        compiler_params=_params(("arbitrary",), 32),
        name="moe_combine_norm",
    )(d1, d2, x, meta, g, y)


def moe_layer_and_final_norm(x, g, router_w, w_gate, w_up, w_down, final_g):
    t, d = x.shape
    n_experts = router_w.shape[1]
    tm = MOE_TILE
    rw = jnp.zeros((d, LANES), F32).at[:, :n_experts].set(router_w)
    h, meta, cnt = moe_router(x, g, rw, n_experts)

    rows = t * TOP_K + n_experts * tm
    n_tiles = rows // tm
    counts = cnt[0, :n_experts].astype(I32)
    tiles_per = (counts + tm - 1) // tm
    tile_end = jnp.cumsum(tiles_per)
    offs = (tile_end - tiles_per) * tm
    n_valid = tile_end[-1:]
    tile_expert = jnp.minimum(
        jnp.searchsorted(tile_end, jnp.arange(n_tiles, dtype=I32), side="right"),
        n_experts - 1).astype(I32)
    i1 = meta[:, META_I1].astype(I32)
    i2 = meta[:, META_I2].astype(I32)
    d1 = offs[i1] + meta[:, META_P1].astype(I32)
    d2 = offs[i2] + meta[:, META_P2].astype(I32)
    tid = jnp.arange(t, dtype=I32)
    tok = jnp.zeros((rows,), I32).at[jnp.concatenate([d1, d2])].set(jnp.concatenate([tid, tid]))

    act = moe_up(h, tok, tile_expert, n_valid, w_gate, w_up)
    y = moe_down(act, tile_expert, n_valid, w_down)
    return moe_combine_norm(x, meta, d1, d2, y, final_g)


def kernel(x, rel_bias, attn_norm_g, w_in, mix_norm_dil_g, mix_norm_sb_g, w_o, ffn_norm_g,
           dense_w_gate, dense_w_up, dense_w_down, router_w, moe_w_gate, moe_w_up,
           moe_w_down, final_norm_g):
    b, s, d = x.shape
    depth = w_in.shape[0]
    assert depth == 2, "layer 0 dense SwiGLU, layer 1 routed experts followed by the final norm"
    d_dil = mix_norm_dil_g.shape[1]
    d_sb = mix_norm_sb_g.shape[1]
    h_dil, h_sb = d_dil // HEAD_DIM, d_sb // HEAD_DIM
    tables = _dilated_bias_tables(rel_bias)
    row = lambda v: v.reshape(1, -1)

    xt = x.reshape(b * s, d)
    for layer in range(depth):
        w_in_l = w_in[layer].astype(BF16)
        g_attn = row(attn_norm_g[layer])
        p_dil, p_sb = norm_matmul_split(xt, g_attn, w_in_l, 3 * d_dil, F32, BF16)
        o_dil = dilated_attention(p_dil.reshape(b, s, 3 * d_dil), tables, h_dil)
        o_sb = stick_breaking_attention(p_sb.reshape(b, s, 3 * d_sb), h_sb)
        xt = mix_out(o_dil.reshape(b * s, d_dil),
                     tuple(o.reshape(b * s // 2, d_sb) for o in o_sb),
                     row(mix_norm_dil_g[layer]), row(mix_norm_sb_g[layer]),
                     w_o[layer].astype(BF16), xt, s)
        g_ffn = row(ffn_norm_g[layer])
        j = layer // 2
        if layer % 2 == 0:
            act = ffn_up(xt, g_ffn, dense_w_gate[j].astype(BF16), dense_w_up[j].astype(BF16))
            xt = matmul_residual(act, dense_w_down[j].astype(BF16), xt)
        else:
            xt = moe_layer_and_final_norm(
                xt, g_ffn, router_w[j], moe_w_gate[j].astype(BF16), moe_w_up[j].astype(BF16),
                moe_w_down[j].astype(BF16), row(final_norm_g))
    return xt.reshape(b, s, d)
```

```python
import functools

import numpy as np
import jax
import jax.numpy as jnp
from jax import lax
from jax.experimental import pallas as pl
from jax.experimental.pallas import tpu as pltpu

F32 = jnp.float32
BF16 = jnp.bfloat16
I32 = jnp.int32

EPS = 1e-6
HEAD_DIM = 128
DILATED_BRANCHES = ((128, 1), (512, 4), (2048, 16))
N_BUCKETS = 32
MAX_DISTANCE = 2048
TOP_K = 2
LANES = 128
WIN = 128
NEG_INF = float("-inf")
DIL_UNROLL = 8
MIB = 1024 * 1024


def _params(semantics, vmem_mib):
    return pltpu.CompilerParams(dimension_semantics=semantics,
                                vmem_limit_bytes=vmem_mib * MIB)


def _rms(x, g):
    return x * lax.rsqrt(jnp.mean(x * x, axis=-1, keepdims=True) + EPS) * g


def _norm_matmul_body(n_first, x_ref, g_ref, w_ref, oa_ref, ob_ref, h_ref):
    j = pl.program_id(1)

    @pl.when(j == 0)
    def _():
        h_ref[...] = _rms(x_ref[...], g_ref[...]).astype(BF16)

    y = jnp.dot(h_ref[...], w_ref[...], preferred_element_type=F32)

    @pl.when(j < n_first)
    def _():
        oa_ref[...] = y.astype(oa_ref.dtype)

    @pl.when(j >= n_first)
    def _():
        ob_ref[...] = y.astype(ob_ref.dtype)


def norm_matmul_split(x, g, w, n_a, dtype_a, dtype_b, tm=1024, tn=1024):
    t, d = x.shape
    n = w.shape[1]
    tm, tn = min(tm, t), min(tn, n_a, n - n_a)
    assert n_a % tn == 0 and (n - n_a) % tn == 0
    ja = n_a // tn
    return pl.pallas_call(
        functools.partial(_norm_matmul_body, ja),
        grid=(t // tm, n // tn),
        in_specs=[pl.BlockSpec((tm, d), lambda i, j: (i, 0)),
                  pl.BlockSpec((1, d), lambda i, j: (0, 0)),
                  pl.BlockSpec((d, tn), lambda i, j: (0, j))],
        out_specs=[pl.BlockSpec((tm, tn), lambda i, j: (i, jnp.minimum(j, ja - 1))),
                   pl.BlockSpec((tm, tn), lambda i, j: (i, jnp.maximum(j - ja, 0)))],
        out_shape=[jax.ShapeDtypeStruct((t, n_a), dtype_a),
                   jax.ShapeDtypeStruct((t, n - n_a), dtype_b)],
        scratch_shapes=[pltpu.VMEM((tm, d), BF16)],
        compiler_params=_params(("parallel", "arbitrary"), 56),
        name="norm_matmul",
    )(x, g, w)


def _silu_mul(a, b):
    return a * (1.0 / (1.0 + jnp.exp(-a))) * b


def _ffn_up_body(x_ref, g_ref, wg_ref, wu_ref, o_ref, h_ref):
    @pl.when(pl.program_id(1) == 0)
    def _():
        h_ref[...] = _rms(x_ref[...], g_ref[...]).astype(BF16)

    h = h_ref[...]
    a = jnp.dot(h, wg_ref[...], preferred_element_type=F32)
    b = jnp.dot(h, wu_ref[...], preferred_element_type=F32)
    o_ref[...] = _silu_mul(a, b).astype(o_ref.dtype)


def ffn_up(x, g, wg, wu, tm=1024, tn=512):
    t, d = x.shape
    f = wg.shape[1]
    tm, tn = min(tm, t), min(tn, f)
    return pl.pallas_call(
        _ffn_up_body,
        grid=(t // tm, pl.cdiv(f, tn)),
        in_specs=[pl.BlockSpec((tm, d), lambda i, j: (i, 0)),
                  pl.BlockSpec((1, d), lambda i, j: (0, 0)),
                  pl.BlockSpec((d, tn), lambda i, j: (0, j)),
                  pl.BlockSpec((d, tn), lambda i, j: (0, j))],
        out_specs=pl.BlockSpec((tm, tn), lambda i, j: (i, j)),
        out_shape=jax.ShapeDtypeStruct((t, f), BF16),
        scratch_shapes=[pltpu.VMEM((tm, d), BF16)],
        compiler_params=_params(("parallel", "arbitrary"), 56),
        name="ffn_up",
    )(x, g, wg, wu)


def _matmul_res_body(a_ref, w_ref, r_ref, o_ref):
    o_ref[...] = r_ref[...] + jnp.dot(a_ref[...], w_ref[...], preferred_element_type=F32)


def matmul_residual(a, w, res, tm=512, tn=1024):
    t, f = a.shape
    n = w.shape[1]
    tm, tn = min(tm, t), min(tn, n)
    return pl.pallas_call(
        _matmul_res_body,
        grid=(n // tn, t // tm),
        in_specs=[pl.BlockSpec((tm, f), lambda j, i: (i, 0)),
                  pl.BlockSpec((f, tn), lambda j, i: (0, j)),
                  pl.BlockSpec((tm, tn), lambda j, i: (i, j))],
        out_specs=pl.BlockSpec((tm, tn), lambda j, i: (i, j)),
        out_shape=jax.ShapeDtypeStruct((t, n), F32),
        compiler_params=_params(("parallel", "parallel"), 52),
        name="matmul_residual",
    )(a, w, res)


def _mix_out_body(tiles_per_seq, od_ref, sa_ref, sb_ref, gd_ref, gs_ref, w_ref, x_ref, o_ref, h_ref):
    dd = od_ref.shape[1]
    half = tiles_per_seq // 2
    si = pl.program_id(0) % tiles_per_seq

    @pl.when(pl.program_id(1) == 0)
    def _():
        h_ref[:, :dd] = _rms(od_ref[...].astype(F32), gd_ref[...]).astype(BF16)

        @pl.when(si < half)
        def _():
            h_ref[:, dd:] = _rms(sa_ref[...].astype(F32), gs_ref[...]).astype(BF16)

        @pl.when(si >= half)
        def _():
            h_ref[:, dd:] = _rms(sb_ref[...].astype(F32), gs_ref[...]).astype(BF16)

    o_ref[...] = x_ref[...] + jnp.dot(h_ref[...], w_ref[...], preferred_element_type=F32)


def mix_out(o_dil, o_sb_halves, g_dil, g_sb, w_o, x, seq, tm=512, tn=2048):
    t, dd = o_dil.shape
    sa, sb = o_sb_halves
    ds = sa.shape[1]
    n = w_o.shape[1]
    tm, tn = min(tm, seq // 2), min(tn, n)
    tps = seq // tm
    half = tps // 2
    return pl.pallas_call(
        functools.partial(_mix_out_body, tps),
        grid=(t // tm, n // tn),
        in_specs=[pl.BlockSpec((tm, dd), lambda i, j: (i, 0)),
                  pl.BlockSpec((tm, ds), lambda i, j: ((i // tps) * half + jnp.minimum(i % tps, half - 1), 0)),
                  pl.BlockSpec((tm, ds), lambda i, j: ((i // tps) * half + jnp.maximum(i % tps - half, 0), 0)),
                  pl.BlockSpec((1, dd), lambda i, j: (0, 0)),
                  pl.BlockSpec((1, ds), lambda i, j: (0, 0)),
                  pl.BlockSpec((dd + ds, tn), lambda i, j: (0, j)),
                  pl.BlockSpec((tm, tn), lambda i, j: (i, j))],
        out_specs=pl.BlockSpec((tm, tn), lambda i, j: (i, j)),
        out_shape=jax.ShapeDtypeStruct((t, n), F32),
        scratch_shapes=[pltpu.VMEM((tm, dd + ds), BF16)],
        compiler_params=_params(("parallel", "arbitrary"), 56),
        name="mix_out",
    )(o_dil, sa, sb, g_dil, g_sb, w_o, x)


def _t5_bucket(dist):
    n = np.asarray(dist, dtype=np.int64)
    max_exact = N_BUCKETS // 2
    large = max_exact + (np.log(np.maximum(n, 1) / max_exact)
                         / np.log(MAX_DISTANCE / max_exact)
                         * (N_BUCKETS - max_exact)).astype(np.int64)
    large = np.minimum(large, N_BUCKETS - 1)
    return np.where(n < max_exact, n, large).astype(np.int32)


def _dilated_bias_tables(rel_bias):
    n_heads = rel_bias.shape[1]
    period = 3 * WIN
    pad = jnp.full((n_heads, period - WIN - 1), NEG_INF, F32)
    tables = []
    for (w, d) in DILATED_BRANCHES:
        assert w // d == WIN
        bias_m = rel_bias[_t5_bucket(d * np.arange(WIN, -1, -1))].astype(F32).T
        u = jnp.concatenate([bias_m, pad], axis=1)
        shifted = jnp.tile(u, (1, WIN))[:, :WIN * (period - 1)].reshape(n_heads, WIN, period - 1)
        tables.append(shifted[:, :, :2 * WIN])
    return jnp.stack(tables, axis=1)


def _dilated_body(q_ref, k_ref, v_ref, tb_ref, o_ref, ob_ref, lb_ref):
    s = q_ref.shape[0]
    scale = HEAD_DIM ** -0.5

    def load(ref, pos, d):
        return ref[pl.ds(pos, WIN, stride=d), :]

    def attend(bi, d, base, q, kk, vv, tb):
        sc = lax.dot_general(q, kk, (((1,), (1,)), ((), ())), preferred_element_type=F32) + tb
        m = jnp.max(sc, axis=-1, keepdims=True)
        p = jnp.exp(sc - m)
        l = jnp.sum(p, axis=-1, keepdims=True)
        out = jnp.dot(p.astype(BF16), vv, preferred_element_type=F32) * (1.0 / l)
        lse = m + jnp.log(l)
        ob_ref[bi, pl.ds(base, WIN, stride=d), :] = out
        lb_ref[bi, pl.ds(base, WIN, stride=d), :] = jnp.broadcast_to(lse, (WIN, LANES))

    def run(bi, d, start, nb, k_prev, v_prev):
        for b in range(nb):
            base = start + d * WIN * b
            q = (load(q_ref, base, d) * scale).astype(BF16)
            k_cur = load(k_ref, base, d).astype(BF16)
            v_cur = load(v_ref, base, d).astype(BF16)
            if k_prev is None:
                attend(bi, d, base, q, k_cur, v_cur, tb_ref[bi, :, WIN:])
            else:
                attend(bi, d, base, q, jnp.concatenate([k_prev, k_cur], axis=0),
                       jnp.concatenate([v_prev, v_cur], axis=0), tb_ref[bi])
            k_prev, v_prev = k_cur, v_cur

    for bi, (_, d) in enumerate(DILATED_BRANCHES):
        nblk = s // (d * WIN)
        if nblk <= DIL_UNROLL:
            per = DIL_UNROLL // nblk

            def trip(g, carry, bi=bi, d=d, nblk=nblk, per=per):
                for u in range(per):
                    run(bi, d, g * per + u, nblk, None, None)
                return carry

            lax.fori_loop(0, d // per, trip, 0)
        else:
            chunks = nblk // DIL_UNROLL
            span = d * WIN * DIL_UNROLL

            def head(r, carry, bi=bi, d=d):
                run(bi, d, r, DIL_UNROLL, None, None)
                return carry

            def tail(it, carry, bi=bi, d=d, chunks=chunks, span=span):
                start = it // (chunks - 1) + span * (it % (chunks - 1) + 1)
                run(bi, d, start, DIL_UNROLL, load(k_ref, start - d * WIN, d).astype(BF16),
                    load(v_ref, start - d * WIN, d).astype(BF16))
                return carry

            lax.fori_loop(0, d, head, 0)
            lax.fori_loop(0, d * (chunks - 1), tail, 0)

    ch = 256
    def merge(c, carry):
        rows = pl.ds(pl.multiple_of(c * ch, ch), ch)
        ls = [lb_ref[b, rows, :] for b in range(len(DILATED_BRANCHES))]
        mx = functools.reduce(jnp.maximum, ls)
        es = [jnp.exp(x - mx) for x in ls]
        inv = 1.0 / functools.reduce(jnp.add, es)
        acc = es[0] * inv * ob_ref[0, rows, :]
        for b in range(1, len(DILATED_BRANCHES)):
            acc = acc + es[b] * inv * ob_ref[b, rows, :]
        o_ref[rows, :] = acc.astype(o_ref.dtype)
        return carry

    lax.fori_loop(0, s // ch, merge, 0)


def dilated_attention(proj, tables, n_heads):
    b, s, _ = proj.shape
    h = n_heads
    nbr = len(DILATED_BRANCHES)
    return pl.pallas_call(
        _dilated_body,
        grid=(b, h),
        in_specs=[pl.BlockSpec((None, s, HEAD_DIM), lambda bi, hi: (bi, 0, hi)),
                  pl.BlockSpec((None, s, HEAD_DIM), lambda bi, hi: (bi, 0, h + hi)),
                  pl.BlockSpec((None, s, HEAD_DIM), lambda bi, hi: (bi, 0, 2 * h + hi)),
                  pl.BlockSpec((None, nbr, WIN, 2 * WIN), lambda bi, hi: (hi, 0, 0, 0))],
        out_specs=pl.BlockSpec((None, s, HEAD_DIM), lambda bi, hi: (bi, 0, hi)),
        out_shape=jax.ShapeDtypeStruct((b, s, h * HEAD_DIM), BF16),
        scratch_shapes=[pltpu.VMEM((nbr, s, HEAD_DIM), F32),
                        pltpu.VMEM((nbr, s, LANES), F32)],
        compiler_params=_params(("parallel", "parallel"), 48),
        name="dilated_attention",
    )(proj, proj, proj, tables)


SB_BLOCK = 256
LOG2E = 1.4426950408889634


def _softplus2(z2):
    return jnp.maximum(z2, 0.0) + jnp.log2(1.0 + jnp.exp2(-jnp.abs(z2)))


def _sb_body(nq, qa_ref, qb_ref, k_ref, v_ref, oa_ref, ob_ref, q_ref, acc_ref, c_ref):
    blk = SB_BLOCK
    p = pl.program_id(2)
    row = lax.broadcasted_iota(I32, (blk, blk), 0)
    col = lax.broadcasted_iota(I32, (blk, blk), 1)
    tri = (row >= col).astype(BF16)
    causal = col < row
    for slot, src in enumerate((qa_ref, qb_ref)):
        q_ref[slot] = (src[...].astype(F32) * (HEAD_DIM ** -0.5 * LOG2E)).astype(BF16)

    work = [(0, p, True), (1, nq - 1 - p, True)]
    for t in range(nq - 1):
        first = t < p
        work.append((jnp.where(first, 0, 1), jnp.where(first, p - 1 - t, nq - 2 - t), False))
    zs, vs, sums, cs = {}, {}, {}, {}

    def stage_scores(i):
        slot, kb, _ = work[i]
        rows = pl.ds(pl.multiple_of(kb * blk, blk), blk)
        zs[i] = lax.dot_general(q_ref[slot], k_ref[rows, :], (((1,), (1,)), ((), ())),
                                preferred_element_type=F32)
        vs[i] = v_ref[rows, :]

    def stage_sums(i):
        slot, _, diag = work[i]
        sp = _softplus2(zs[i])
        if diag:
            sp = jnp.where(causal, sp, 0.0)
        total = jnp.broadcast_to(jnp.sum(sp, axis=-1, keepdims=True), (blk, LANES))
        if diag:
            c_ref[slot] = total
        else:
            cs[i] = c_ref[slot]
            c_ref[slot] = cs[i] + total
        sums[i] = jnp.dot(sp.astype(BF16), tri, preferred_element_type=F32)

    def stage_out(i):
        slot, _, diag = work[i]
        e = zs.pop(i) - sums.pop(i)
        if diag:
            a = jnp.where(causal, jnp.exp2(e), 0.0)
        else:
            c = cs.pop(i)
            a = jnp.exp2(e - jnp.concatenate([c, c], axis=1))
        pv = jnp.dot(a.astype(BF16), vs.pop(i), preferred_element_type=F32)
        if diag:
            acc_ref[slot] = pv
        else:
            acc_ref[slot] += pv

    n = len(work)
    for step in range(n + 2):
        if step < n:
            stage_scores(step)
        if 0 <= step - 1 < n:
            stage_sums(step - 1)
        if 0 <= step - 2 < n:
            stage_out(step - 2)
    oa_ref[...] = acc_ref[0].astype(oa_ref.dtype)
    ob_ref[...] = acc_ref[1].astype(ob_ref.dtype)


def stick_breaking_attention(proj, n_heads):
    b, s, _ = proj.shape
    h = n_heads
    blk = SB_BLOCK
    nq = s // blk
    assert LANES * 2 == blk and nq % 2 == 0
    q_spec = lambda f: pl.BlockSpec((None, blk, HEAD_DIM), lambda bi, hi, p: (bi, f(p), hi))
    oa, ob = pl.pallas_call(
        functools.partial(_sb_body, nq),
        grid=(b, h, nq // 2),
        in_specs=[q_spec(lambda p: p), q_spec(lambda p: nq - 1 - p),
                  pl.BlockSpec((None, s, HEAD_DIM), lambda bi, hi, p: (bi, 0, h + hi)),
                  pl.BlockSpec((None, s, HEAD_DIM), lambda bi, hi, p: (bi, 0, 2 * h + hi))],
        out_specs=[q_spec(lambda p: p), q_spec(lambda p: nq // 2 - 1 - p)],
        out_shape=[jax.ShapeDtypeStruct((b, s // 2, h * HEAD_DIM), BF16)] * 2,
        scratch_shapes=[pltpu.VMEM((2, blk, HEAD_DIM), BF16),
                        pltpu.VMEM((2, blk, HEAD_DIM), F32),
                        pltpu.VMEM((2, blk, LANES), F32)],
        compiler_params=_params(("parallel", "parallel", "parallel"), 32),
        name="stick_breaking_attention",
    )(proj, proj, proj, proj)
    return oa, ob


MOE_TILE = 512
DMA_ISSUE_UNROLL = 8
META_I1, META_I2, META_W1, META_W2, META_P1, META_P2 = range(6)


def _router_body(n_experts, x_ref, g_ref, rw_ref, h_ref, meta_ref, cnt_ref, carry_ref):
    tm = x_ref.shape[0]

    @pl.when(pl.program_id(0) == 0)
    def _():
        carry_ref[...] = jnp.zeros_like(carry_ref)

    h = _rms(x_ref[...], g_ref[...])
    h_ref[...] = h
    rw = rw_ref[...]
    h_hi, rw_hi = h.astype(BF16), rw.astype(BF16)
    h_lo = (h - h_hi.astype(F32)).astype(BF16)
    rw_lo = (rw - rw_hi.astype(F32)).astype(BF16)
    logits = (jnp.dot(h_hi, rw_hi, preferred_element_type=F32)
              + (jnp.dot(h_lo, rw_hi, preferred_element_type=F32)
                 + jnp.dot(h_hi, rw_lo, preferred_element_type=F32)))
    lane = lax.broadcasted_iota(I32, logits.shape, 1)
    l1 = jnp.where(lane < n_experts, logits, NEG_INF)
    m1 = jnp.max(l1, axis=-1, keepdims=True)
    i1 = jnp.min(jnp.where(l1 == m1, lane, LANES), axis=-1, keepdims=True)
    l2 = jnp.where(lane == i1, NEG_INF, l1)
    m2 = jnp.max(l2, axis=-1, keepdims=True)
    i2 = jnp.min(jnp.where(l2 == m2, lane, LANES), axis=-1, keepdims=True)
    e2 = jnp.exp(m2 - m1)
    inv = 1.0 / (1.0 + e2)
    w1, w2 = inv, e2 * inv
    sel1, sel2 = lane == i1, lane == i2
    onehot = jnp.where(sel1 | sel2, 1.0, 0.0)
    r = lax.broadcasted_iota(I32, (tm, tm), 0)
    c = lax.broadcasted_iota(I32, (tm, tm), 1)
    before = (c < r).astype(BF16)
    pos = jnp.dot(before, onehot.astype(BF16), preferred_element_type=F32) + carry_ref[...]
    p1 = jnp.sum(jnp.where(sel1, pos, 0.0), axis=-1, keepdims=True)
    p2 = jnp.sum(jnp.where(sel2, pos, 0.0), axis=-1, keepdims=True)
    meta = jnp.zeros_like(logits)
    for idx, val in ((META_I1, i1.astype(F32)), (META_I2, i2.astype(F32)), (META_W1, w1),
                     (META_W2, w2), (META_P1, p1), (META_P2, p2)):
        meta = jnp.where(lane == idx, val, meta)
    meta_ref[...] = meta
    carry_ref[...] += jnp.sum(onehot, axis=0, keepdims=True)
    cnt_ref[...] = carry_ref[...]


def moe_router(x, g, rw, n_experts, tm=512):
    t, d = x.shape
    tm = min(tm, t)
    return pl.pallas_call(
        functools.partial(_router_body, n_experts),
        grid=(t // tm,),
        in_specs=[pl.BlockSpec((tm, d), lambda i: (i, 0)),
                  pl.BlockSpec((1, d), lambda i: (0, 0)),
                  pl.BlockSpec((d, LANES), lambda i: (0, 0))],
        out_specs=[pl.BlockSpec((tm, d), lambda i: (i, 0)),
                   pl.BlockSpec((tm, LANES), lambda i: (i, 0)),
                   pl.BlockSpec((1, LANES), lambda i: (0, 0))],
        out_shape=[jax.ShapeDtypeStruct((t, d), F32),
                   jax.ShapeDtypeStruct((t, LANES), F32),
                   jax.ShapeDtypeStruct((1, LANES), F32)],
        scratch_shapes=[pltpu.VMEM((1, LANES), F32)],
        compiler_params=_params(("arbitrary",), 32),
        name="moe_router",
    )(x, g, rw)


def _row_copy(src_hbm, src_row, dst_ref, dst_row, sem):
    return pltpu.make_async_copy(src_hbm.at[pl.ds(src_row, 1), :],
                                 dst_ref.at[pl.ds(dst_row, 1), :], sem)


def _moe_up_body(te_ref, nv_ref, tok_ref, h_hbm, wg_ref, wu_ref, o_ref, x_ref, h_ref, sem):
    i, j = pl.program_id(0), pl.program_id(1)
    tm = h_ref.shape[0]
    n_valid = nv_ref[0]

    def start_gather(tile, slot):
        base = tile * tm

        def issue(r, carry):
            _row_copy(h_hbm, tok_ref[base + r], x_ref.at[slot], r, sem.at[slot]).start()
            return carry

        lax.fori_loop(0, tm, issue, 0, unroll=DMA_ISSUE_UNROLL)

    @pl.when((i == 0) & (j == 0))
    def _():
        start_gather(0, 0)

    @pl.when(i < n_valid)
    def _():
        @pl.when(j == 0)
        def _():
            slot = i % 2
            pltpu.make_async_copy(h_hbm.at[pl.ds(0, tm), :], x_ref.at[slot], sem.at[slot]).wait()
            h_ref[...] = x_ref[slot].astype(BF16)

            @pl.when(i + 1 < n_valid)
            def _():
                start_gather(i + 1, 1 - slot)

        h = h_ref[...]
        a = jnp.dot(h, wg_ref[...], preferred_element_type=F32)
        b = jnp.dot(h, wu_ref[...], preferred_element_type=F32)
        o_ref[...] = _silu_mul(a, b).astype(o_ref.dtype)

    @pl.when(i >= nv_ref[0])
    def _():
        o_ref[...] = jnp.zeros_like(o_ref)


def moe_up(h, tok, tile_expert, n_valid, wg, wu, tn=1792):
    d = h.shape[1]
    rows = tok.shape[0]
    f = wg.shape[2]
    tm, tn = MOE_TILE, min(tn, f)
    return pl.pallas_call(
        _moe_up_body,
        grid_spec=pltpu.PrefetchScalarGridSpec(
            num_scalar_prefetch=3,
            grid=(rows // tm, f // tn),
            in_specs=[pl.BlockSpec(memory_space=pl.ANY),
                      pl.BlockSpec((None, d, tn), lambda i, j, te, nv, tok: (te[i], 0, j)),
                      pl.BlockSpec((None, d, tn), lambda i, j, te, nv, tok: (te[i], 0, j))],
            out_specs=pl.BlockSpec((tm, tn), lambda i, j, te, nv, tok: (i, j)),
            scratch_shapes=[pltpu.VMEM((2, tm, d), F32),
                            pltpu.VMEM((tm, d), BF16),
                            pltpu.SemaphoreType.DMA((2,))]),
        out_shape=jax.ShapeDtypeStruct((rows, f), BF16),
        compiler_params=_params(("arbitrary", "arbitrary"), 56),
        name="moe_up",
    )(tile_expert, n_valid, tok, h, wg, wu)


def _moe_down_body(te_ref, nv_ref, a_ref, w_ref, o_ref):
    i = pl.program_id(0)

    @pl.when(i < nv_ref[0])
    def _():
        o_ref[...] = jnp.dot(a_ref[...], w_ref[...], preferred_element_type=F32)

    @pl.when(i >= nv_ref[0])
    def _():
        o_ref[...] = jnp.zeros_like(o_ref)


def moe_down(act, tile_expert, n_valid, wd, tn=512):
    rows, f = act.shape
    d = wd.shape[2]
    tm, tn = MOE_TILE, min(tn, d)
    return pl.pallas_call(
        _moe_down_body,
        grid_spec=pltpu.PrefetchScalarGridSpec(
            num_scalar_prefetch=2,
            grid=(rows // tm, d // tn),
            in_specs=[pl.BlockSpec((tm, f), lambda i, j, te, nv: (i, 0)),
                      pl.BlockSpec((None, f, tn), lambda i, j, te, nv: (te[i], 0, j))],
            out_specs=pl.BlockSpec((tm, tn), lambda i, j, te, nv: (i, j))),
        out_shape=jax.ShapeDtypeStruct((rows, d), F32),
        compiler_params=_params(("parallel", "arbitrary"), 56),
        name="moe_down",
    )(tile_expert, n_valid, act, wd)


def _combine_body(d1_ref, d2_ref, x_ref, meta_ref, g_ref, y_hbm, o_ref, buf_ref, sem):
    tc = x_ref.shape[0]
    base = pl.program_id(0) * tc

    def issue(r, carry):
        _row_copy(y_hbm, d1_ref[base + r], buf_ref.at[0], r, sem).start()
        _row_copy(y_hbm, d2_ref[base + r], buf_ref.at[1], r, sem).start(priority=1)
        return carry

    lax.fori_loop(0, tc, issue, 0, unroll=DMA_ISSUE_UNROLL)
    for k in range(TOP_K):
        pltpu.make_async_copy(y_hbm.at[pl.ds(0, tc), :], buf_ref.at[k], sem).wait()
    meta = meta_ref[...]
    w1 = meta[:, META_W1:META_W1 + 1]
    w2 = meta[:, META_W2:META_W2 + 1]
    y = x_ref[...] + (w1 * buf_ref[0] + w2 * buf_ref[1])
    o_ref[...] = _rms(y, g_ref[...])


def moe_combine_norm(x, meta, d1, d2, y, g, tc=256):
    t, d = x.shape
    tc = min(tc, t)
    return pl.pallas_call(
        _combine_body,
        grid_spec=pltpu.PrefetchScalarGridSpec(
            num_scalar_prefetch=2,
            grid=(t // tc,),
            in_specs=[pl.BlockSpec((tc, d), lambda i, a, b: (i, 0)),
                      pl.BlockSpec((tc, LANES), lambda i, a, b: (i, 0)),
                      pl.BlockSpec((1, d), lambda i, a, b: (0, 0)),
                      pl.BlockSpec(memory_space=pl.ANY)],
            out_specs=pl.BlockSpec((tc, d), lambda i, a, b: (i, 0)),
            scratch_shapes=[pltpu.VMEM((2, tc, d), F32),
                            pltpu.SemaphoreType.DMA(())]),
        out_shape=jax.ShapeDtypeStruct((t, d), F32),
        compiler_params=_params(("arbitrary",), 32),
        name="moe_combine_norm",
    )(d1, d2, x, meta, g, y)


def moe_layer_and_final_norm(x, g, router_w, w_gate, w_up, w_down, final_g):
    t, d = x.shape
    n_experts = router_w.shape[1]
    tm = MOE_TILE
    rw = jnp.zeros((d, LANES), F32).at[:, :n_experts].set(router_w)
    h, meta, cnt = moe_router(x, g, rw, n_experts)

    rows = t * TOP_K + n_experts * tm
    n_tiles = rows // tm
    counts = cnt[0, :n_experts].astype(I32)
    tiles_per = (counts + tm - 1) // tm
    tile_end = jnp.cumsum(tiles_per)
    offs = (tile_end - tiles_per) * tm
    n_valid = tile_end[-1:]
    tile_expert = jnp.minimum(
        jnp.searchsorted(tile_end, jnp.arange(n_tiles, dtype=I32), side="right"),
        n_experts - 1).astype(I32)
    i1 = meta[:, META_I1].astype(I32)
    i2 = meta[:, META_I2].astype(I32)
    d1 = offs[i1] + meta[:, META_P1].astype(I32)
    d2 = offs[i2] + meta[:, META_P2].astype(I32)
    tid = jnp.arange(t, dtype=I32)
    tok = jnp.zeros((rows,), I32).at[jnp.concatenate([d1, d2])].set(jnp.concatenate([tid, tid]))

    act = moe_up(h, tok, tile_expert, n_valid, w_gate, w_up)
    y = moe_down(act, tile_expert, n_valid, w_down)
    return moe_combine_norm(x, meta, d1, d2, y, final_g)


def kernel(x, rel_bias, attn_norm_g, w_in, mix_norm_dil_g, mix_norm_sb_g, w_o, ffn_norm_g,
           dense_w_gate, dense_w_up, dense_w_down, router_w, moe_w_gate, moe_w_up,
           moe_w_down, final_norm_g):
    b, s, d = x.shape
    depth = w_in.shape[0]
    assert depth == 2, "layer 0 dense SwiGLU, layer 1 routed experts followed by the final norm"
    d_dil = mix_norm_dil_g.shape[1]
    d_sb = mix_norm_sb_g.shape[1]
    h_dil, h_sb = d_dil // HEAD_DIM, d_sb // HEAD_DIM
    tables = _dilated_bias_tables(rel_bias)
    row = lambda v: v.reshape(1, -1)

    xt = x.reshape(b * s, d)
    for layer in range(depth):
        w_in_l = w_in[layer].astype(BF16)
        g_attn = row(attn_norm_g[layer])
        p_dil, p_sb = norm_matmul_split(xt, g_attn, w_in_l, 3 * d_dil, F32, BF16)
        o_dil = dilated_attention(p_dil.reshape(b, s, 3 * d_dil), tables, h_dil)
        o_sb = stick_breaking_attention(p_sb.reshape(b, s, 3 * d_sb), h_sb)
        xt = mix_out(o_dil.reshape(b * s, d_dil),
                     tuple(o.reshape(b * s // 2, d_sb) for o in o_sb),
                     row(mix_norm_dil_g[layer]), row(mix_norm_sb_g[layer]),
                     w_o[layer].astype(BF16), xt, s)
        g_ffn = row(ffn_norm_g[layer])
        j = layer // 2
        if layer % 2 == 0:
            act = ffn_up(xt, g_ffn, dense_w_gate[j].astype(BF16), dense_w_up[j].astype(BF16))
            xt = matmul_residual(act, dense_w_down[j].astype(BF16), xt)
        else:
            xt = moe_layer_and_final_norm(
                xt, g_ffn, router_w[j], moe_w_gate[j].astype(BF16), moe_w_up[j].astype(BF16),
                moe_w_down[j].astype(BF16), row(final_norm_g))
    return xt.reshape(b, s, d)
```
